```python
import jax, jax.numpy as jnp
from jax import lax
import numpy as np

D_MODEL = 1024
BATCH = 4
SEQ = 4096
DEPTH = 4

N_MIXERS = 3
RMS_EPS = 1e-6
BLOCK = 128
NEG = -1e30

SWA_HEAD_DIM = 64
SWA_HEADS = D_MODEL // SWA_HEAD_DIM
SWA_KV_HEADS = SWA_HEADS // 8
SWA_GROUP = SWA_HEADS // SWA_KV_HEADS
WINDOW = 128

SB_HEAD_DIM = 64
SB_HEADS = D_MODEL // SB_HEAD_DIM

RET_QK_DIM = 256
RET_HEADS = D_MODEL // RET_QK_DIM
RET_V_DIM = 2 * RET_QK_DIM

D_FF = 4 * D_MODEL

N_A = len(range(0, DEPTH, N_MIXERS))
N_B = len(range(1, DEPTH, N_MIXERS))
N_C = len(range(2, DEPTH, N_MIXERS))

kernel_name = "hybrid_swa_stickbreak_retnet_trunk"


def rms_norm(x, g):
    xf = x.astype(jnp.float32)
    y = xf * lax.rsqrt(jnp.mean(xf * xf, axis=-1, keepdims=True) + RMS_EPS)
    return (y * g.astype(jnp.float32)).astype(x.dtype)


def alibi_slopes(n_heads):
    return jnp.exp2(-8.0 * jnp.arange(1, n_heads + 1, dtype=jnp.float32) / n_heads)


def swa_mixer(h, w_qkv, sinks, w_o):
    B_, T, _ = h.shape
    nb = T // BLOCK
    HD, KV, G = SWA_HEAD_DIM, SWA_KV_HEADS, SWA_GROUP
    qkv = h @ w_qkv
    q, k, v = jnp.split(qkv, [SWA_HEADS * HD, (SWA_HEADS + KV) * HD], axis=-1)
    q = q.reshape(B_, nb, BLOCK, KV, G, HD)
    k = k.reshape(B_, nb, BLOCK, KV, HD)
    v = v.reshape(B_, nb, BLOCK, KV, HD)

    def with_prev(a):
        prev = jnp.pad(a, ((0, 0), (1, 0), (0, 0), (0, 0), (0, 0)))[:, :-1]
        return jnp.concatenate([prev, a], axis=2)

    kb, vb = with_prev(k), with_prev(v)
    s = jnp.einsum('bnqkgd,bnskd->bnkgqs', q, kb).astype(jnp.float32) * (HD ** -0.5)

    blk = jnp.arange(nb)[:, None, None]
    qi = jnp.arange(BLOCK)[None, :, None]
    kj = jnp.arange(2 * BLOCK)[None, None, :]
    dist = BLOCK + qi - kj
    kpos = (blk - 1) * BLOCK + kj
    mask = (dist >= 0) & (dist < WINDOW) & (kpos >= 0)
    slopes = alibi_slopes(SWA_HEADS).reshape(1, 1, KV, G, 1, 1)
    bias = -slopes * dist.astype(jnp.float32)[None, :, None, None]
    logits = jnp.where(mask[None, :, None, None], s + bias, NEG)

    sink = sinks.astype(jnp.float32).reshape(1, 1, KV, G, 1, 1)
    m = jnp.maximum(jnp.max(logits, axis=-1, keepdims=True), sink)
    p = jnp.exp(logits - m)
    probs = p / (jnp.sum(p, axis=-1, keepdims=True) + jnp.exp(sink - m))
    o = jnp.einsum('bnkgqs,bnskd->bnqkgd', probs.astype(vb.dtype), vb)
    return o.reshape(B_, T, SWA_HEADS * HD) @ w_o


def stick_breaking_mixer(h, w_qkv, w_o):
    B_, T, _ = h.shape
    HD, H = SB_HEAD_DIM, SB_HEADS
    qkv = (h @ w_qkv).reshape(B_, T, 3, H, HD)
    q = jnp.transpose(qkv[:, :, 0], (0, 2, 1, 3))
    k = jnp.transpose(qkv[:, :, 1], (0, 2, 1, 3))
    v = jnp.transpose(qkv[:, :, 2], (0, 2, 1, 3))
    scale = HD ** -0.5
    outs = []
    for n in range(T // BLOCK):
        q0, end = n * BLOCK, (n + 1) * BLOCK
        z = jnp.einsum('bhqd,bhsd->bhqs', q[:, :, q0:end], k[:, :, :end]).astype(jnp.float32) * scale
        t_pos = q0 + jnp.arange(BLOCK)[:, None]
        s_pos = jnp.arange(end)[None, :]
        causal = (s_pos < t_pos)[None, None]
        log_beta = jax.nn.log_sigmoid(z)
        log_1m_beta = jnp.where(causal, jax.nn.log_sigmoid(-z), 0.0)
        rest = lax.cumsum(log_1m_beta, axis=3, reverse=True) - log_1m_beta
        a = jnp.where(causal, jnp.exp(log_beta + rest), 0.0)
        outs.append(jnp.einsum('bhqs,bhsd->bhqd', a.astype(v.dtype), v[:, :, :end]))
    o = jnp.concatenate(outs, axis=2)
    return jnp.transpose(o, (0, 2, 1, 3)).reshape(B_, T, H * HD) @ w_o


def retention_mixer(h, w_in, w_o):
    B_, T, _ = h.shape
    nc, C, H, dk, dv = T // BLOCK, BLOCK, RET_HEADS, RET_QK_DIM, RET_V_DIM
    proj = h @ w_in
    q, k, v, g = jnp.split(proj, [H * dk, 2 * H * dk, 2 * H * dk + H * dv], axis=-1)
    q = q.astype(jnp.float32).reshape(B_, nc, C, H, dk)
    k = k.astype(jnp.float32).reshape(B_, nc, C, H, dk) * (dk ** -0.5)
    v = v.astype(jnp.float32).reshape(B_, nc, C, H, dv)

    log_gamma = jnp.log1p(-jnp.exp2(-5.0 - jnp.arange(H, dtype=jnp.float32)))
    idx = jnp.arange(C)
    diff = idx[:, None] - idx[None, :]
    decay_mat = jnp.where(diff >= 0,
                          jnp.exp(log_gamma[:, None, None] * jnp.maximum(diff, 0).astype(jnp.float32)),
                          0.0)
    scores = jnp.einsum('bnihd,bnjhd->bnhij', q, k) * decay_mat
    o_inner = jnp.einsum('bnhij,bnjhe->bnihe', scores, v)
    q_decay = jnp.exp(log_gamma[:, None] * (idx + 1).astype(jnp.float32))
    k_decay = jnp.exp(log_gamma[:, None] * (C - 1 - idx).astype(jnp.float32))
    chunk_decay = jnp.exp(log_gamma * C)[None, :, None, None]
    kv_chunk = jnp.einsum('bnjhd,hj,bnjhe->bnhde', k, k_decay, v)

    def step(state, kv_n):
        return state * chunk_decay + kv_n, state

    init = jnp.zeros((B_, H, dk, dv), jnp.float32)
    _, prev_states = lax.scan(step, init, jnp.moveaxis(kv_chunk, 1, 0))
    prev_states = jnp.moveaxis(prev_states, 0, 1)
    o_cross = jnp.einsum('bnihd,hi,bnhde->bnihe', q, q_decay, prev_states)
    o = o_inner + o_cross
    o = o * lax.rsqrt(jnp.mean(o * o, axis=-1, keepdims=True) + RMS_EPS)
    o = o.reshape(B_, T, H * dv)
    y = jax.nn.silu(g.astype(jnp.float32)) * o
    return y.astype(h.dtype) @ w_o


def squared_relu_mlp(h, w_up, w_down):
    a = jax.nn.relu(h @ w_up)
    return (a * a) @ w_down


def setup_inputs(seed: int = 0) -> dict:
    key = jax.random.key(seed)
    ks = jax.random.split(key, 16)

    def dense(k, shape):
        return jax.random.normal(k, shape, jnp.float32) * (shape[-2] ** -0.5)

    def gain(k, shape):
        return 1.0 + 0.02 * jax.random.normal(k, shape, jnp.float32)

    swa_qkv_dim = (SWA_HEADS + 2 * SWA_KV_HEADS) * SWA_HEAD_DIM
    ret_in_dim = 2 * RET_HEADS * RET_QK_DIM + 2 * RET_HEADS * RET_V_DIM
    return {
        "x": jax.random.normal(ks[0], (BATCH, SEQ, D_MODEL), jnp.float32),
        "attn_norm": gain(ks[1], (DEPTH, D_MODEL)),
        "mlp_norm": gain(ks[2], (DEPTH, D_MODEL)),
        "final_norm": gain(ks[3], (D_MODEL,)),
        "swa_w_qkv": dense(ks[4], (N_A, D_MODEL, swa_qkv_dim)),
        "swa_sinks": 0.5 * jax.random.normal(ks[5], (N_A, SWA_HEADS), jnp.float32),
        "swa_w_o": dense(ks[6], (N_A, SWA_HEADS * SWA_HEAD_DIM, D_MODEL)),
        "sb_w_qkv": dense(ks[7], (N_B, D_MODEL, 3 * SB_HEADS * SB_HEAD_DIM)),
        "sb_w_o": dense(ks[8], (N_B, SB_HEADS * SB_HEAD_DIM, D_MODEL)),
        "ret_w_in": dense(ks[9], (N_C, D_MODEL, ret_in_dim)),
        "ret_w_o": dense(ks[10], (N_C, RET_HEADS * RET_V_DIM, D_MODEL)),
        "mlp_w_up": dense(ks[11], (DEPTH, D_MODEL, D_FF)),
        "mlp_w_down": dense(ks[12], (DEPTH, D_FF, D_MODEL)),
    }


def reference(x, attn_norm, mlp_norm, final_norm, swa_w_qkv, swa_sinks, swa_w_o,
              sb_w_qkv, sb_w_o, ret_w_in, ret_w_o, mlp_w_up, mlp_w_down):
    h = x
    for i in range(DEPTH):
        kind, j = i % N_MIXERS, i // N_MIXERS
        u = rms_norm(h, attn_norm[i])
        if kind == 0:
            mix = swa_mixer(u, swa_w_qkv[j], swa_sinks[j], swa_w_o[j])
        elif kind == 1:
            mix = stick_breaking_mixer(u, sb_w_qkv[j], sb_w_o[j])
        else:
            mix = retention_mixer(u, ret_w_in[j], ret_w_o[j])
        h = h + mix
        h = h + squared_relu_mlp(rms_norm(h, mlp_norm[i]), mlp_w_up[i], mlp_w_down[i])
    return rms_norm(h, final_norm)
```

```python
import functools

import jax
import jax.numpy as jnp
from jax import lax
from jax.experimental import pallas as pl
from jax.experimental.pallas import tpu as pltpu

D_MODEL = 1024
DEPTH = 4
N_MIXERS = 3
RMS_EPS = 1e-6
BLOCK = 128
NEG = -1e30

SWA_HEAD_DIM = 64
SWA_HEADS = 16
SWA_KV_HEADS = 2
SWA_GROUP = 8
SB_HEAD_DIM = 64
SB_HEADS = 16
RET_QK_DIM = 256
RET_HEADS = 4
RET_V_DIM = 512
D_FF = 4 * D_MODEL

LANES = 128
VMEM_LIMIT = 48 * 1024 * 1024

F32 = jnp.float32
BF16 = jnp.bfloat16


def _params(*sem):
    return pltpu.CompilerParams(dimension_semantics=sem, vmem_limit_bytes=VMEM_LIMIT)


def _rms(x, g):
    return x * lax.rsqrt(jnp.mean(x * x, axis=-1, keepdims=True) + RMS_EPS) * g


def _norm_proj_kernel(x_ref, g_ref, w_ref, o_ref, xn_ref):
    @pl.when(pl.program_id(1) == 0)
    def _():
        xn_ref[...] = _rms(x_ref[...], g_ref[...]).astype(BF16)

    o_ref[...] = jnp.dot(xn_ref[...], w_ref[...], preferred_element_type=F32).astype(o_ref.dtype)


def norm_proj(x, g, w, *, tm, tn, out_dtype=BF16):
    n, d = x.shape
    dout = w.shape[1]
    return pl.pallas_call(
        _norm_proj_kernel,
        out_shape=jax.ShapeDtypeStruct((n, dout), out_dtype),
        grid=(n // tm, dout // tn),
        in_specs=[
            pl.BlockSpec((tm, d), lambda i, j: (i, 0)),
            pl.BlockSpec((1, d), lambda i, j: (0, 0)),
            pl.BlockSpec((d, tn), lambda i, j: (0, j)),
        ],
        out_specs=pl.BlockSpec((tm, tn), lambda i, j: (i, j)),
        scratch_shapes=[pltpu.VMEM((tm, d), BF16)],
        compiler_params=_params("parallel", "arbitrary"),
        name="norm_proj",
    )(x, g.reshape(1, d), w)


def _proj_residual_kernel(y_ref, w_ref, h_ref, o_ref):
    o_ref[...] = h_ref[...] + jnp.dot(y_ref[...], w_ref[...], preferred_element_type=F32)


def proj_residual(y, w, h, *, tm):
    n, k = y.shape
    d = w.shape[1]
    return pl.pallas_call(
        _proj_residual_kernel,
        out_shape=jax.ShapeDtypeStruct((n, d), F32),
        grid=(n // tm,),
        in_specs=[
            pl.BlockSpec((tm, k), lambda i: (i, 0)),
            pl.BlockSpec((k, d), lambda i: (0, 0)),
            pl.BlockSpec((tm, d), lambda i: (i, 0)),
        ],
        out_specs=pl.BlockSpec((tm, d), lambda i: (i, 0)),
        compiler_params=_params("parallel"),
        name="proj_residual",
    )(y, w, h)


def _mlp_kernel(x_ref, g_ref, wu_ref, wd_ref, o_ref, xn_ref):
    j = pl.program_id(1)

    @pl.when(j == 0)
    def _():
        x = x_ref[...]
        xn_ref[...] = _rms(x, g_ref[...]).astype(BF16)
        o_ref[...] = x

    a = jnp.maximum(jnp.dot(xn_ref[...], wu_ref[...], preferred_element_type=F32), 0.0)
    o_ref[...] += jnp.dot((a * a).astype(BF16), wd_ref[...], preferred_element_type=F32)


def mlp(x, g, w_up, w_down, *, tm, tf):
    n, d = x.shape
    dff = w_up.shape[1]
    return pl.pallas_call(
        _mlp_kernel,
        out_shape=jax.ShapeDtypeStruct((n, d), F32),
        grid=(n // tm, dff // tf),
        in_specs=[
            pl.BlockSpec((tm, d), lambda i, j: (i, 0)),
            pl.BlockSpec((1, d), lambda i, j: (0, 0)),
            pl.BlockSpec((d, tf), lambda i, j: (0, j)),
            pl.BlockSpec((tf, d), lambda i, j: (j, 0)),
        ],
        out_specs=pl.BlockSpec((tm, d), lambda i, j: (i, 0)),
        scratch_shapes=[pltpu.VMEM((tm, d), BF16)],
        compiler_params=_params("parallel", "arbitrary"),
        name="mlp",
    )(x, g.reshape(1, d), w_up, w_down)


def _final_norm_kernel(x_ref, g_ref, o_ref):
    o_ref[...] = _rms(x_ref[...], g_ref[...])


def final_rmsnorm(x, g, *, tm):
    n, d = x.shape
    return pl.pallas_call(
        _final_norm_kernel,
        out_shape=jax.ShapeDtypeStruct((n, d), F32),
        grid=(n // tm,),
        in_specs=[pl.BlockSpec((tm, d), lambda i: (i, 0)), pl.BlockSpec((1, d), lambda i: (0, 0))],
        out_specs=pl.BlockSpec((tm, d), lambda i: (i, 0)),
        compiler_params=_params("parallel"),
        name="final_norm",
    )(x, g.reshape(1, d))


def _swa_kernel(sink_ref, q_ref, kvp_ref, kvc_ref, o_ref):
    n = pl.program_id(1)
    q = q_ref[...]
    kv = jnp.concatenate([kvp_ref[...], kvc_ref[...]], axis=0)
    qi = lax.broadcasted_iota(jnp.int32, (BLOCK, 2 * BLOCK), 0)
    kj = lax.broadcasted_iota(jnp.int32, (BLOCK, 2 * BLOCK), 1)
    dist = BLOCK + qi - kj
    mask = (dist >= 0) & (dist < BLOCK) & ((kj >= BLOCK) | (n > 0))
    distf = dist.astype(F32)
    low = lax.broadcasted_iota(jnp.int32, (BLOCK, LANES), 1) < SWA_HEAD_DIM
    scale = SWA_HEAD_DIM ** -0.5
    for g in range(SWA_KV_HEADS):
        kk = kv[:, g * LANES:(g + 1) * LANES]
        vv = kv[:, (SWA_KV_HEADS + g) * LANES:(SWA_KV_HEADS + g + 1) * LANES]
        for p in range(SWA_GROUP // 2):
            h0 = g * SWA_GROUP + 2 * p
            q2 = q[:, h0 * SWA_HEAD_DIM:h0 * SWA_HEAD_DIM + LANES]
            outs = []
            for half in range(2):
                h = h0 + half
                slope = 2.0 ** (-8.0 * (h + 1) / SWA_HEADS)
                qm = jnp.where(low if half == 0 else jnp.logical_not(low), q2, jnp.zeros_like(q2))
                s = lax.dot_general(qm, kk, (((1,), (1,)), ((), ())), preferred_element_type=F32)
                logits = jnp.where(mask, s * scale - slope * distf, NEG)
                sink = sink_ref[h]
                m = jnp.maximum(jnp.max(logits, axis=-1, keepdims=True), sink)
                pexp = jnp.exp(logits - m)
                denom = jnp.sum(pexp, axis=-1, keepdims=True) + jnp.exp(sink - m)
                probs = pexp / denom
                outs.append(jnp.dot(probs.astype(BF16), vv, preferred_element_type=F32))
            o_ref[:, h0 * SWA_HEAD_DIM:h0 * SWA_HEAD_DIM + LANES] = jnp.where(low, outs[0], outs[1]).astype(o_ref.dtype)


def swa_attention(qkv, sinks, *, batch, seq):
    nb = seq // BLOCK
    dq = SWA_HEADS * SWA_HEAD_DIM
    dkv = 4 * LANES
    kv_col = dq // dkv
    return pl.pallas_call(
        _swa_kernel,
        out_shape=jax.ShapeDtypeStruct((batch * seq, dq), BF16),
        grid=(batch, nb),
        in_specs=[
            pl.BlockSpec(memory_space=pltpu.SMEM),
            pl.BlockSpec((BLOCK, dq), lambda b, n: (b * nb + n, 0)),
            pl.BlockSpec((BLOCK, dkv), lambda b, n: (b * nb + jnp.maximum(n - 1, 0), kv_col)),
            pl.BlockSpec((BLOCK, dkv), lambda b, n: (b * nb + n, kv_col)),
        ],
        out_specs=pl.BlockSpec((BLOCK, dq), lambda b, n: (b * nb + n, 0)),
        compiler_params=_params("parallel", "arbitrary"),
        name="swa_attention",
    )(sinks, qkv, qkv, qkv)


def _swa_weight(w_qkv):
    dq = SWA_HEADS * SWA_HEAD_DIM
    hd = SWA_HEAD_DIM
    wq = w_qkv[:, :dq]
    wk = w_qkv[:, dq:dq + SWA_KV_HEADS * hd].reshape(D_MODEL, SWA_KV_HEADS, 1, hd)
    wv = w_qkv[:, dq + SWA_KV_HEADS * hd:].reshape(D_MODEL, SWA_KV_HEADS, 1, hd)
    dup = lambda w: jnp.broadcast_to(w, (D_MODEL, SWA_KV_HEADS, 2, hd)).reshape(D_MODEL, 2 * SWA_KV_HEADS * hd)
    return jnp.concatenate([wq, dup(wk), dup(wv)], axis=1)


def _sb_kernel(q_ref, k_ref, v_ref, o_ref):
    n = pl.program_id(2)
    q2 = q_ref[...] * jnp.asarray(SB_HEAD_DIM ** -0.5, BF16)
    low = lax.broadcasted_iota(jnp.int32, (BLOCK, LANES), 1) < SB_HEAD_DIM
    zero = jnp.zeros_like(q2)
    qm = (jnp.where(low, q2, zero), jnp.where(low, zero, q2))
    tj = lax.broadcasted_iota(jnp.int32, (BLOCK, 2 * BLOCK), 0)
    ts = lax.broadcasted_iota(jnp.int32, (BLOCK, 2 * BLOCK), 1)
    tri = jnp.where((tj > ts) | (ts >= BLOCK), 1.0, 0.0).astype(BF16)
    ti = lax.broadcasted_iota(jnp.int32, (BLOCK, BLOCK), 0)
    si = lax.broadcasted_iota(jnp.int32, (BLOCK, BLOCK), 1)
    causal = si < ti

    def block(kb, carry, diag):
        start = pl.multiple_of(kb * BLOCK, BLOCK)
        k2 = k_ref[pl.ds(start, BLOCK), :]
        v2 = v_ref[pl.ds(start, BLOCK), :]
        new = []
        for half in range(2):
            run, acc = carry[half]
            z = lax.dot_general(qm[half], k2, (((1,), (1,)), ((), ())), preferred_element_type=F32)
            log_beta = jnp.minimum(z, 0.0) - jnp.log(1.0 + jnp.exp(-jnp.abs(z)))
            log_1m = log_beta - z
            if diag:
                log_1m = jnp.where(causal, log_1m, 0.0)
            hi = log_1m.astype(BF16)
            lo = (log_1m - hi.astype(F32)).astype(BF16)
            c = jnp.dot(hi, tri, preferred_element_type=F32) + jnp.dot(lo, tri, preferred_element_type=F32)
            a = jnp.exp(log_beta + c[:, :BLOCK] + run)
            if diag:
                a = jnp.where(causal, a, 0.0)
            acc = acc + jnp.dot(a.astype(BF16), v2, preferred_element_type=F32)
            new.append((run + c[:, BLOCK:], acc))
        return tuple(new)

    zf = jnp.zeros((BLOCK, LANES), F32)
    carry = block(n, ((zf, zf), (zf, zf)), True)
    carry = lax.fori_loop(0, n, lambda i, c: block(n - 1 - i, c, False), carry)
    o_ref[...] = jnp.where(low, carry[0][1], carry[1][1]).astype(o_ref.dtype)


def sb_attention(qkv, *, batch, seq):
    nb = seq // BLOCK
    pairs = SB_HEADS * SB_HEAD_DIM // LANES
    return pl.pallas_call(
        _sb_kernel,
        out_shape=jax.ShapeDtypeStruct((batch * seq, SB_HEADS * SB_HEAD_DIM), BF16),
        grid=(batch, pairs, nb),
        in_specs=[
            pl.BlockSpec((BLOCK, LANES), lambda b, p, n: (b * nb + n, p)),
            pl.BlockSpec((seq, LANES), lambda b, p, n: (b, pairs + p)),
            pl.BlockSpec((seq, LANES), lambda b, p, n: (b, 2 * pairs + p)),
        ],
        out_specs=pl.BlockSpec((BLOCK, LANES), lambda b, p, n: (b * nb + n, p)),
        compiler_params=_params("parallel", "parallel", "arbitrary"),
        name="sb_attention",
    )(qkv, qkv, qkv)


def _ret_kernel(q_ref, k_ref, v_ref, g_ref, dm_ref, qd_ref, kd_ref, cd_ref, y_ref, state_ref):
    @pl.when(pl.program_id(2) == 0)
    def _():
        state_ref[...] = jnp.zeros_like(state_ref)

    q = q_ref[...]
    kf = k_ref[...].astype(F32) * (RET_QK_DIM ** -0.5)
    v = v_ref[...]
    scores = lax.dot_general(q, kf.astype(BF16), (((1,), (1,)), ((), ())), preferred_element_type=F32) * dm_ref[0]
    state = state_ref[...]
    qd = (q.astype(F32) * qd_ref[0]).astype(BF16)
    o = jnp.dot(scores.astype(BF16), v, preferred_element_type=F32) + jnp.dot(qd, state.astype(BF16), preferred_element_type=F32)
    kd = (kf * kd_ref[0]).astype(BF16)
    state_ref[...] = state * cd_ref[0] + lax.dot_general(kd, v, (((0,), (0,)), ((), ())), preferred_element_type=F32)
    o = o * lax.rsqrt(jnp.mean(o * o, axis=-1, keepdims=True) + RMS_EPS)
    gate = g_ref[...]
    y_ref[...] = (gate * (1.0 / (1.0 + jnp.exp(-gate))) * o).astype(y_ref.dtype)


def _ret_decay_tables():
    c, h = BLOCK, RET_HEADS
    log_gamma = jnp.log1p(-jnp.exp2(-5.0 - jnp.arange(h, dtype=F32)))
    idx = jnp.arange(c)
    diff = idx[:, None] - idx[None, :]
    decay_mat = jnp.where(diff >= 0, jnp.exp(log_gamma[:, None, None] * jnp.maximum(diff, 0).astype(F32)), 0.0)
    q_decay = jnp.exp(log_gamma[:, None] * (idx + 1).astype(F32))[:, :, None]
    k_decay = jnp.exp(log_gamma[:, None] * (c - 1 - idx).astype(F32))[:, :, None]
    chunk_decay = jnp.exp(log_gamma * c)[:, None, None]
    return decay_mat, q_decay, k_decay, chunk_decay


def retention(qkv, gate, *, batch, seq):
    nc = seq // BLOCK
    hh, dk, dv = RET_HEADS, RET_QK_DIM, RET_V_DIM
    dm, qd, kd, cd = _ret_decay_tables()
    v_col0 = 2 * hh * dk // dv
    return pl.pallas_call(
        _ret_kernel,
        out_shape=jax.ShapeDtypeStruct((batch * seq, hh * dv), BF16),
        grid=(batch, hh, nc),
        in_specs=[
            pl.BlockSpec((BLOCK, dk), lambda b, h, n: (b * nc + n, h)),
            pl.BlockSpec((BLOCK, dk), lambda b, h, n: (b * nc + n, hh + h)),
            pl.BlockSpec((BLOCK, dv), lambda b, h, n: (b * nc + n, v_col0 + h)),
            pl.BlockSpec((BLOCK, dv), lambda b, h, n: (b * nc + n, h)),
            pl.BlockSpec((1, BLOCK, BLOCK), lambda b, h, n: (h, 0, 0)),
            pl.BlockSpec((1, BLOCK, 1), lambda b, h, n: (h, 0, 0)),
            pl.BlockSpec((1, BLOCK, 1), lambda b, h, n: (h, 0, 0)),
            pl.BlockSpec((1, 1, 1), lambda b, h, n: (h, 0, 0)),
        ],
        out_specs=pl.BlockSpec((BLOCK, dv), lambda b, h, n: (b * nc + n, h)),
        scratch_shapes=[pltpu.VMEM((dk, dv), F32)],
        compiler_params=_params("parallel", "parallel", "arbitrary"),
        name="retention",
    )(qkv, qkv, qkv, gate, dm, qd, kd, cd)


def kernel(x, attn_norm, mlp_norm, final_norm, swa_w_qkv, swa_sinks, swa_w_o, sb_w_qkv, sb_w_o, ret_w_in, ret_w_o,
           mlp_w_up, mlp_w_down):
    batch, seq, d = x.shape
    h = x.reshape(batch * seq, d)
    tm = 512
    ret_qkv_dim = 2 * RET_HEADS * RET_QK_DIM + RET_HEADS * RET_V_DIM
    for i in range(DEPTH):
        kind, j = i % N_MIXERS, i // N_MIXERS
        if kind == 0:
            qkv = norm_proj(h, attn_norm[i], _swa_weight(swa_w_qkv[j]).astype(BF16), tm=tm, tn=512)
            mix = swa_attention(qkv, swa_sinks[j], batch=batch, seq=seq)
            w_o = swa_w_o[j]
        elif kind == 1:
            qkv = norm_proj(h, attn_norm[i], sb_w_qkv[j].astype(BF16), tm=tm, tn=1024)
            mix = sb_attention(qkv, batch=batch, seq=seq)
            w_o = sb_w_o[j]
        else:
            w_in = ret_w_in[j]
            qkv = norm_proj(h, attn_norm[i], w_in[:, :ret_qkv_dim].astype(BF16), tm=tm, tn=1024)
            gate = norm_proj(h, attn_norm[i], w_in[:, ret_qkv_dim:].astype(BF16), tm=tm, tn=1024, out_dtype=F32)
            mix = retention(qkv, gate, batch=batch, seq=seq)
            w_o = ret_w_o[j]
        h = proj_residual(mix, w_o.astype(BF16), h, tm=tm)
        h = mlp(h, mlp_norm[i], mlp_w_up[i].astype(BF16), mlp_w_down[i].astype(BF16), tm=tm, tf=1024)
    return final_rmsnorm(h, final_norm, tm=tm).reshape(batch, seq, d)
```

```python
import functools

import jax
import jax.numpy as jnp
from jax import lax
from jax.experimental import pallas as pl
from jax.experimental.pallas import tpu as pltpu

D_MODEL = 1024
DEPTH = 4
N_MIXERS = 3
RMS_EPS = 1e-6
BLOCK = 128
NEG = -1e30

SWA_HEAD_DIM = 64
SWA_HEADS = 16
SWA_KV_HEADS = 2
SWA_GROUP = 8
SB_HEAD_DIM = 64
SB_HEADS = 16
SB_TILE = 512
RET_QK_DIM = 256
RET_HEADS = 4
RET_V_DIM = 512
D_FF = 4 * D_MODEL

LANES = 128
VMEM_LIMIT = 48 * 1024 * 1024
TOKEN_TILE = 512
COL_CHUNK = 1024
FF_TILE = 1024
LOG2E = 1.4426950408889634

F32 = jnp.float32
BF16 = jnp.bfloat16


def _params(*sem):
    return pltpu.CompilerParams(dimension_semantics=sem, vmem_limit_bytes=VMEM_LIMIT)


def _rms(x, g):
    return x * lax.rsqrt(jnp.mean(x * x, axis=-1, keepdims=True) + RMS_EPS) * g


def _resident(shape):
    return pl.BlockSpec(shape, lambda *_: (0,) * len(shape), pipeline_mode=pl.Buffered(1))


def _norm_proj_kernel(n_out, x_ref, g_ref, *refs):
    w_refs, o_refs = refs[:n_out], refs[n_out:]
    xn = _rms(x_ref[...], g_ref[...]).astype(BF16)
    for w_ref, o_ref in zip(w_refs, o_refs):
        for c in range(0, w_ref.shape[1], COL_CHUNK):
            ce = min(c + COL_CHUNK, w_ref.shape[1])
            o_ref[:, c:ce] = jnp.dot(xn, w_ref[:, c:ce], preferred_element_type=F32).astype(o_ref.dtype)


def norm_proj(x, g, ws, dtypes):
    n, d = x.shape
    tm = TOKEN_TILE
    return pl.pallas_call(
        functools.partial(_norm_proj_kernel, len(ws)),
        out_shape=[jax.ShapeDtypeStruct((n, w.shape[1]), dt) for w, dt in zip(ws, dtypes)],
        grid=(n // tm,),
        in_specs=[pl.BlockSpec((tm, d), lambda i: (i, 0)), _resident((1, d))] + [_resident(w.shape) for w in ws],
        out_specs=[pl.BlockSpec((tm, w.shape[1]), lambda i: (i, 0)) for w in ws],
        compiler_params=_params("parallel"),
        name="norm_proj",
    )(x, g.reshape(1, d), *ws)


def _tail_kernel(final, mix_ref, wo_ref, h_ref, g_ref, wu_ref, wd_ref, *refs):
    if final:
        fg_ref, o_ref, xn_ref = refs
    else:
        o_ref, xn_ref = refs
    j = pl.program_id(1)

    @pl.when(j == 0)
    def _():
        h1 = h_ref[...] + jnp.dot(mix_ref[...], wo_ref[...], preferred_element_type=F32)
        xn_ref[...] = _rms(h1, g_ref[...]).astype(BF16)
        o_ref[...] = h1

    a = jnp.maximum(jnp.dot(xn_ref[...], wu_ref[...], preferred_element_type=F32), 0.0)
    o_ref[...] += jnp.dot((a * a).astype(BF16), wd_ref[...], preferred_element_type=F32)

    if final:
        @pl.when(j == pl.num_programs(1) - 1)
        def _():
            o_ref[...] = _rms(o_ref[...], fg_ref[...])


def layer_tail(mix, w_o, h, g, w_up, w_down, final_gain=None):
    n, d = h.shape
    k = mix.shape[1]
    dff = w_up.shape[1]
    tm, tf = TOKEN_TILE, FF_TILE
    final = final_gain is not None
    in_specs = [
        pl.BlockSpec((tm, k), lambda i, j: (i, 0)),
        _resident((k, d)),
        pl.BlockSpec((tm, d), lambda i, j: (i, 0)),
        _resident((1, d)),
        pl.BlockSpec((d, tf), lambda i, j: (0, j)),
        pl.BlockSpec((tf, d), lambda i, j: (j, 0)),
    ]
    args = [mix, w_o, h, g.reshape(1, d), w_up, w_down]
    if final:
        in_specs.append(_resident((1, d)))
        args.append(final_gain.reshape(1, d))
    return pl.pallas_call(
        functools.partial(_tail_kernel, final),
        out_shape=jax.ShapeDtypeStruct((n, d), F32),
        grid=(n // tm, dff // tf),
        in_specs=in_specs,
        out_specs=pl.BlockSpec((tm, d), lambda i, j: (i, 0)),
        scratch_shapes=[pltpu.VMEM((tm, d), BF16)],
        compiler_params=_params("parallel", "arbitrary"),
        name="layer_tail",
    )(*args)


def _swa_kernel(sink_ref, q_ref, kvp_ref, kvc_ref, o_ref):
    n = pl.program_id(1)
    q = q_ref[...] * jnp.asarray(SWA_HEAD_DIM ** -0.5, BF16)
    kv = jnp.concatenate([kvp_ref[...], kvc_ref[...]], axis=0)
    qi = lax.broadcasted_iota(jnp.int32, (BLOCK, BLOCK), 0)
    ci = lax.broadcasted_iota(jnp.int32, (BLOCK, BLOCK), 1)
    upper = ci > qi
    valid = jnp.logical_not(upper) | (n > 0)
    distf = jnp.where(upper, BLOCK + qi - ci, qi - ci).astype(F32)
    low = lax.broadcasted_iota(jnp.int32, (BLOCK, LANES), 1) < SWA_HEAD_DIM
    for g in range(SWA_KV_HEADS):
        kk = kv[:, g * LANES:(g + 1) * LANES]
        vv = kv[:, (SWA_KV_HEADS + g) * LANES:(SWA_KV_HEADS + g + 1) * LANES]
        for p in range(SWA_GROUP // 2):
            h0 = g * SWA_GROUP + 2 * p
            q2 = q[:, h0 * SWA_HEAD_DIM:h0 * SWA_HEAD_DIM + LANES]
            outs = []
            for half in range(2):
                h = h0 + half
                slope = 2.0 ** (-8.0 * (h + 1) / SWA_HEADS)
                qm = jnp.where(low if half == 0 else jnp.logical_not(low), q2, jnp.zeros_like(q2))
                s = lax.dot_general(qm, kk, (((1,), (1,)), ((), ())), preferred_element_type=F32)
                s = jnp.where(upper, s[:, :BLOCK], s[:, BLOCK:])
                logits = jnp.where(valid, s - slope * distf, NEG)
                sink = sink_ref[h]
                m = jnp.maximum(jnp.max(logits, axis=-1, keepdims=True), sink)
                pexp = jnp.exp(logits - m)
                inv = 1.0 / (jnp.sum(pexp, axis=-1, keepdims=True) + jnp.exp(sink - m))
                pb = pexp.astype(BF16)
                zero = jnp.zeros_like(pb)
                p2 = jnp.concatenate([jnp.where(upper, pb, zero), jnp.where(upper, zero, pb)], axis=1)
                outs.append(jnp.dot(p2, vv, preferred_element_type=F32) * inv)
            o_ref[:, h0 * SWA_HEAD_DIM:h0 * SWA_HEAD_DIM + LANES] = jnp.where(low, outs[0], outs[1]).astype(o_ref.dtype)


def swa_attention(qkv, sinks, *, batch, seq):
    nb = seq // BLOCK
    dq = SWA_HEADS * SWA_HEAD_DIM
    dkv = 2 * SWA_KV_HEADS * LANES
    kv_col = dq // dkv
    return pl.pallas_call(
        _swa_kernel,
        out_shape=jax.ShapeDtypeStruct((batch * seq, dq), BF16),
        grid=(batch, nb),
        in_specs=[
            pl.BlockSpec(memory_space=pltpu.SMEM),
            pl.BlockSpec((BLOCK, dq), lambda b, n: (b * nb + n, 0)),
            pl.BlockSpec((BLOCK, dkv), lambda b, n: (b * nb + jnp.maximum(n - 1, 0), kv_col)),
            pl.BlockSpec((BLOCK, dkv), lambda b, n: (b * nb + n, kv_col)),
        ],
        out_specs=pl.BlockSpec((BLOCK, dq), lambda b, n: (b * nb + n, 0)),
        compiler_params=_params("parallel", "arbitrary"),
        name="swa_attention",
    )(sinks, qkv, qkv, qkv)


def _swa_weight(w_qkv):
    dq = SWA_HEADS * SWA_HEAD_DIM
    hd = SWA_HEAD_DIM
    wq = w_qkv[:, :dq]
    wk = w_qkv[:, dq:dq + SWA_KV_HEADS * hd].reshape(D_MODEL, SWA_KV_HEADS, 1, hd)
    wv = w_qkv[:, dq + SWA_KV_HEADS * hd:].reshape(D_MODEL, SWA_KV_HEADS, 1, hd)
    dup = lambda w: jnp.broadcast_to(w, (D_MODEL, SWA_KV_HEADS, 2, hd)).reshape(D_MODEL, 2 * SWA_KV_HEADS * hd)
    return jnp.concatenate([wq, dup(wk), dup(wv)], axis=1)


def _sb_kernel(q_ref, k_ref, v_ref, o_ref, kx_ref, vx_ref, run_ref, acc_ref):
    n = pl.program_id(2)
    nkb = k_ref.shape[0] // BLOCK
    sub = SB_TILE // BLOCK
    low = lax.broadcasted_iota(jnp.int32, (nkb, BLOCK, LANES), 2) < SB_HEAD_DIM

    @pl.when(n == 0)
    def _():
        for src, dst in ((k_ref, kx_ref), (v_ref, vx_ref)):
            x = src[...].reshape(nkb, BLOCK, LANES)
            zero = jnp.zeros_like(x)
            dst[:, :BLOCK, :] = jnp.where(low, x, zero)
            dst[:, BLOCK:, :] = jnp.where(low, zero, x)

    q = q_ref[...] * jnp.asarray(SB_HEAD_DIM ** -0.5, BF16)
    tj = lax.broadcasted_iota(jnp.int32, (2 * BLOCK, 2 * BLOCK), 0) % BLOCK
    ts = lax.broadcasted_iota(jnp.int32, (2 * BLOCK, 2 * BLOCK), 1)
    tri = jnp.where((tj > ts) | (ts >= BLOCK), 1.0, 0.0).astype(BF16)
    ti = lax.broadcasted_iota(jnp.int32, (BLOCK, 2 * BLOCK), 0)
    si = lax.broadcasted_iota(jnp.int32, (BLOCK, 2 * BLOCK), 1) % BLOCK
    causal = si < ti
    run_ref[...] = jnp.zeros_like(run_ref)
    acc_ref[...] = jnp.zeros_like(acc_ref)

    def mask_first_block(x):
        head = jnp.where(causal, x[:BLOCK], 0.0)
        return head if x.shape[0] == BLOCK else jnp.concatenate([head, x[BLOCK:]], axis=0)

    def slab(kb, r0, diag):
        z = lax.dot_general(q[r0:], kx_ref[kb], (((1,), (1,)), ((), ())), preferred_element_type=F32)
        t = jnp.maximum(z, 0.0) + jnp.log(1.0 + jnp.exp2(jnp.abs(z) * -LOG2E))
        log_beta = z - t
        if diag:
            t = mask_first_block(t)
        hi = t.astype(BF16)
        lo = (t - hi.astype(F32)).astype(BF16)
        c = [jnp.dot(jnp.concatenate([hi[:, s:s + BLOCK], lo[:, s:s + BLOCK]], axis=1), tri,
                     preferred_element_type=F32) for s in (0, BLOCK)]
        cum = jnp.concatenate([c[0][:, :BLOCK], c[1][:, :BLOCK]], axis=1)
        tot = jnp.concatenate([c[0][:, BLOCK:], c[1][:, BLOCK:]], axis=1)
        run = run_ref[r0:, :]
        a = jnp.exp(log_beta - cum - run)
        if diag:
            a = mask_first_block(a)
        acc_ref[r0:, :] += jnp.dot(a.astype(BF16), vx_ref[kb], preferred_element_type=F32)
        run_ref[r0:, :] = run + tot

    for j in reversed(range(sub)):
        slab(n * sub + j, j * BLOCK, True)

    def tile(i, carry):
        kb0 = (n - 1 - i) * sub
        for j in reversed(range(sub)):
            slab(kb0 + j, 0, False)
        return carry

    lax.fori_loop(0, n, tile, 0)
    o_ref[...] = acc_ref[...].astype(o_ref.dtype)


def sb_attention(qkv, *, batch, seq):
    nt = seq // SB_TILE
    nkb = seq // BLOCK
    pairs = SB_HEADS * SB_HEAD_DIM // LANES
    return pl.pallas_call(
        _sb_kernel,
        out_shape=jax.ShapeDtypeStruct((batch * seq, SB_HEADS * SB_HEAD_DIM), BF16),
        grid=(batch, pairs, nt),
        in_specs=[
            pl.BlockSpec((SB_TILE, LANES), lambda b, p, n: (b * nt + n, p)),
            pl.BlockSpec((seq, LANES), lambda b, p, n: (b, pairs + p)),
            pl.BlockSpec((seq, LANES), lambda b, p, n: (b, 2 * pairs + p)),
        ],
        out_specs=pl.BlockSpec((SB_TILE, LANES), lambda b, p, n: (b * nt + n, p)),
        scratch_shapes=[
            pltpu.VMEM((nkb, 2 * BLOCK, LANES), BF16),
            pltpu.VMEM((nkb, 2 * BLOCK, LANES), BF16),
            pltpu.VMEM((SB_TILE, 2 * LANES), F32),
            pltpu.VMEM((SB_TILE, LANES), F32),
        ],
        compiler_params=_params("parallel", "parallel", "arbitrary"),
        name="sb_attention",
    )(qkv, qkv, qkv)


def _ret_kernel(cd_ref, qkv_ref, g_ref, dm_ref, qd_ref, kd_ref, y_ref, state_ref):
    hh, dk, dv = RET_HEADS, RET_QK_DIM, RET_V_DIM

    @pl.when(pl.program_id(1) == 0)
    def _():
        state_ref[...] = jnp.zeros_like(state_ref)

    for h in range(hh):
        q = qkv_ref[:, h * dk:(h + 1) * dk]
        kf = qkv_ref[:, (hh + h) * dk:(hh + h + 1) * dk].astype(F32) * (dk ** -0.5)
        v = qkv_ref[:, 2 * hh * dk + h * dv:2 * hh * dk + (h + 1) * dv]
        scores = lax.dot_general(q, kf.astype(BF16), (((1,), (1,)), ((), ())), preferred_element_type=F32) * dm_ref[h]
        state = state_ref[h]
        qd = (q.astype(F32) * qd_ref[h]).astype(BF16)
        o = (jnp.dot(scores.astype(BF16), v, preferred_element_type=F32)
             + jnp.dot(qd, state.astype(BF16), preferred_element_type=F32))
        kd = (kf * kd_ref[h]).astype(BF16)
        state_ref[h] = state * cd_ref[h] + lax.dot_general(kd, v, (((0,), (0,)), ((), ())),
                                                           preferred_element_type=F32)
        o = o * lax.rsqrt(jnp.mean(o * o, axis=-1, keepdims=True) + RMS_EPS)
        gate = g_ref[:, h * dv:(h + 1) * dv]
        y_ref[:, h * dv:(h + 1) * dv] = (gate * (1.0 / (1.0 + jnp.exp(-gate))) * o).astype(y_ref.dtype)


def _ret_decay_tables():
    c, h = BLOCK, RET_HEADS
    log_gamma = jnp.log1p(-jnp.exp2(-5.0 - jnp.arange(h, dtype=F32)))
    idx = jnp.arange(c)
    diff = idx[:, None] - idx[None, :]
    decay_mat = jnp.where(diff >= 0, jnp.exp(log_gamma[:, None, None] * jnp.maximum(diff, 0).astype(F32)), 0.0)
    q_decay = jnp.exp(log_gamma[:, None] * (idx + 1).astype(F32))[:, :, None]
    k_decay = jnp.exp(log_gamma[:, None] * (c - 1 - idx).astype(F32))[:, :, None]
    chunk_decay = jnp.exp(log_gamma * c)
    return decay_mat, q_decay, k_decay, chunk_decay


def retention(qkv, gate, *, batch, seq):
    nc = seq // BLOCK
    hh, dk, dv = RET_HEADS, RET_QK_DIM, RET_V_DIM
    dm, qd, kd, cd = _ret_decay_tables()
    return pl.pallas_call(
        _ret_kernel,
        out_shape=jax.ShapeDtypeStruct((batch * seq, hh * dv), BF16),
        grid=(batch, nc),
        in_specs=[
            pl.BlockSpec(memory_space=pltpu.SMEM),
            pl.BlockSpec((BLOCK, qkv.shape[1]), lambda b, n: (b * nc + n, 0)),
            pl.BlockSpec((BLOCK, hh * dv), lambda b, n: (b * nc + n, 0)),
            _resident(dm.shape),
            _resident(qd.shape),
            _resident(kd.shape),
        ],
        out_specs=pl.BlockSpec((BLOCK, hh * dv), lambda b, n: (b * nc + n, 0)),
        scratch_shapes=[pltpu.VMEM((hh, dk, dv), F32)],
        compiler_params=_params("parallel", "arbitrary"),
        name="retention",
    )(cd, qkv, gate, dm, qd, kd)


def kernel(x, attn_norm, mlp_norm, final_norm, swa_w_qkv, swa_sinks, swa_w_o, sb_w_qkv, sb_w_o, ret_w_in, ret_w_o,
           mlp_w_up, mlp_w_down):
    batch, seq, d = x.shape
    h = x.reshape(batch * seq, d)
    ret_qkv_dim = 2 * RET_HEADS * RET_QK_DIM + RET_HEADS * RET_V_DIM
    for i in range(DEPTH):
        kind, j = i % N_MIXERS, i // N_MIXERS
        if kind == 0:
            qkv, = norm_proj(h, attn_norm[i], [_swa_weight(swa_w_qkv[j]).astype(BF16)], [BF16])
            mix = swa_attention(qkv, swa_sinks[j], batch=batch, seq=seq)
            w_o = swa_w_o[j]
        elif kind == 1:
            qkv, = norm_proj(h, attn_norm[i], [sb_w_qkv[j].astype(BF16)], [BF16])
            mix = sb_attention(qkv, batch=batch, seq=seq)
            w_o = sb_w_o[j]
        else:
            w_in = ret_w_in[j].astype(BF16)
            qkv, gate = norm_proj(h, attn_norm[i], [w_in[:, :ret_qkv_dim], w_in[:, ret_qkv_dim:]], [BF16, F32])
            mix = retention(qkv, gate, batch=batch, seq=seq)
            w_o = ret_w_o[j]
        h = layer_tail(mix, w_o.astype(BF16), h, mlp_norm[i], mlp_w_up[i].astype(BF16), mlp_w_down[i].astype(BF16),
                       final_gain=final_norm if i == DEPTH - 1 else None)
    return h.reshape(batch, seq, d)
```

```python
import functools

import jax
import jax.numpy as jnp
from jax import lax
from jax.experimental import pallas as pl
from jax.experimental.pallas import tpu as pltpu

D_MODEL = 1024
DEPTH = 4
N_MIXERS = 3
RMS_EPS = 1e-6
BLOCK = 128
NEG = -1e30

SWA_HEAD_DIM = 64
SWA_HEADS = 16
SWA_KV_HEADS = 2
SWA_GROUP = 8
SB_HEAD_DIM = 64
SB_HEADS = 16
SB_TILE = 512
SB_EXP_ZERO = 128.0
RET_QK_DIM = 256
RET_HEADS = 4
RET_V_DIM = 512
D_FF = 4 * D_MODEL

LANES = 128
VMEM_LIMIT = 48 * 1024 * 1024
TOKEN_TILE = 512
COL_CHUNK = 1024
FF_TILE = 1024
LOG2E = 1.4426950408889634

F32 = jnp.float32
BF16 = jnp.bfloat16


def _params(*sem):
    return pltpu.CompilerParams(dimension_semantics=sem, vmem_limit_bytes=VMEM_LIMIT)


def _rms(x, g):
    return x * lax.rsqrt(jnp.mean(x * x, axis=-1, keepdims=True) + RMS_EPS) * g


def _resident(shape):
    return pl.BlockSpec(shape, lambda *_: (0,) * len(shape), pipeline_mode=pl.Buffered(1))


def _norm_proj_kernel(n_out, x_ref, g_ref, *refs):
    w_refs, o_refs = refs[:n_out], refs[n_out:]
    xn = _rms(x_ref[...], g_ref[...]).astype(BF16)
    for w_ref, o_ref in zip(w_refs, o_refs):
        for c in range(0, w_ref.shape[1], COL_CHUNK):
            ce = min(c + COL_CHUNK, w_ref.shape[1])
            o_ref[:, c:ce] = jnp.dot(xn, w_ref[:, c:ce], preferred_element_type=F32).astype(o_ref.dtype)


def norm_proj(x, g, ws, dtypes):
    n, d = x.shape
    tm = TOKEN_TILE
    return pl.pallas_call(
        functools.partial(_norm_proj_kernel, len(ws)),
        out_shape=[jax.ShapeDtypeStruct((n, w.shape[1]), dt) for w, dt in zip(ws, dtypes)],
        grid=(n // tm,),
        in_specs=[pl.BlockSpec((tm, d), lambda i: (i, 0)), _resident((1, d))] + [_resident(w.shape) for w in ws],
        out_specs=[pl.BlockSpec((tm, w.shape[1]), lambda i: (i, 0)) for w in ws],
        compiler_params=_params("parallel"),
        name="norm_proj",
    )(x, g.reshape(1, d), *ws)


def _tail_kernel(final, mix_ref, wo_ref, h_ref, g_ref, wu_ref, wd_ref, *refs):
    if final:
        fg_ref, o_ref, xn_ref = refs
    else:
        o_ref, xn_ref = refs
    j = pl.program_id(1)

    @pl.when(j == 0)
    def _():
        h1 = h_ref[...] + jnp.dot(mix_ref[...], wo_ref[...], preferred_element_type=F32)
        xn_ref[...] = _rms(h1, g_ref[...]).astype(BF16)
        o_ref[...] = h1

    a = jnp.maximum(jnp.dot(xn_ref[...], wu_ref[...], preferred_element_type=F32), 0.0)
    o_ref[...] += jnp.dot((a * a).astype(BF16), wd_ref[...], preferred_element_type=F32)

    if final:
        @pl.when(j == pl.num_programs(1) - 1)
        def _():
            o_ref[...] = _rms(o_ref[...], fg_ref[...])


def layer_tail(mix, w_o, h, g, w_up, w_down, final_gain=None):
    n, d = h.shape
    k = mix.shape[1]
    dff = w_up.shape[1]
    tm, tf = TOKEN_TILE, FF_TILE
    final = final_gain is not None
    in_specs = [
        pl.BlockSpec((tm, k), lambda i, j: (i, 0)),
        _resident((k, d)),
        pl.BlockSpec((tm, d), lambda i, j: (i, 0)),
        _resident((1, d)),
        pl.BlockSpec((d, tf), lambda i, j: (0, j)),
        pl.BlockSpec((tf, d), lambda i, j: (j, 0)),
    ]
    args = [mix, w_o, h, g.reshape(1, d), w_up, w_down]
    if final:
        in_specs.append(_resident((1, d)))
        args.append(final_gain.reshape(1, d))
    return pl.pallas_call(
        functools.partial(_tail_kernel, final),
        out_shape=jax.ShapeDtypeStruct((n, d), F32),
        grid=(n // tm, dff // tf),
        in_specs=in_specs,
        out_specs=pl.BlockSpec((tm, d), lambda i, j: (i, 0)),
        scratch_shapes=[pltpu.VMEM((tm, d), BF16)],
        compiler_params=_params("parallel", "arbitrary"),
        name="layer_tail",
    )(*args)


def _swa_kernel(sink_ref, q_ref, kvp_ref, kvc_ref, o_ref):
    n = pl.program_id(1)
    q = q_ref[...] * jnp.asarray(SWA_HEAD_DIM ** -0.5, BF16)
    kv = jnp.concatenate([kvp_ref[...], kvc_ref[...]], axis=0)
    qi = lax.broadcasted_iota(jnp.int32, (BLOCK, BLOCK), 0)
    ci = lax.broadcasted_iota(jnp.int32, (BLOCK, BLOCK), 1)
    upper = ci > qi
    valid = jnp.logical_not(upper) | (n > 0)
    distf = jnp.where(upper, BLOCK + qi - ci, qi - ci).astype(F32)
    low = lax.broadcasted_iota(jnp.int32, (BLOCK, LANES), 1) < SWA_HEAD_DIM
    for g in range(SWA_KV_HEADS):
        kk = kv[:, g * LANES:(g + 1) * LANES]
        vv = kv[:, (SWA_KV_HEADS + g) * LANES:(SWA_KV_HEADS + g + 1) * LANES]
        for p in range(SWA_GROUP // 2):
            h0 = g * SWA_GROUP + 2 * p
            q2 = q[:, h0 * SWA_HEAD_DIM:h0 * SWA_HEAD_DIM + LANES]
            outs = []
            for half in range(2):
                h = h0 + half
                slope = 2.0 ** (-8.0 * (h + 1) / SWA_HEADS)
                qm = jnp.where(low if half == 0 else jnp.logical_not(low), q2, jnp.zeros_like(q2))
                s = lax.dot_general(qm, kk, (((1,), (1,)), ((), ())), preferred_element_type=F32)
                s = jnp.where(upper, s[:, :BLOCK], s[:, BLOCK:])
                logits = jnp.where(valid, s - slope * distf, NEG)
                sink = sink_ref[h]
                m = jnp.maximum(jnp.max(logits, axis=-1, keepdims=True), sink)
                pexp = jnp.exp(logits - m)
                inv = 1.0 / (jnp.sum(pexp, axis=-1, keepdims=True) + jnp.exp(sink - m))
                pb = pexp.astype(BF16)
                zero = jnp.zeros_like(pb)
                p2 = jnp.concatenate([jnp.where(upper, pb, zero), jnp.where(upper, zero, pb)], axis=1)
                outs.append(jnp.dot(p2, vv, preferred_element_type=F32) * inv)
            o_ref[:, h0 * SWA_HEAD_DIM:h0 * SWA_HEAD_DIM + LANES] = jnp.where(low, outs[0], outs[1]).astype(o_ref.dtype)


def swa_attention(qkv, sinks, *, batch, seq):
    nb = seq // BLOCK
    dq = SWA_HEADS * SWA_HEAD_DIM
    dkv = 2 * SWA_KV_HEADS * LANES
    kv_col = dq // dkv
    return pl.pallas_call(
        _swa_kernel,
        out_shape=jax.ShapeDtypeStruct((batch * seq, dq), BF16),
        grid=(batch, nb),
        in_specs=[
            pl.BlockSpec(memory_space=pltpu.SMEM),
            pl.BlockSpec((BLOCK, dq), lambda b, n: (b * nb + n, 0)),
            pl.BlockSpec((BLOCK, dkv), lambda b, n: (b * nb + jnp.maximum(n - 1, 0), kv_col)),
            pl.BlockSpec((BLOCK, dkv), lambda b, n: (b * nb + n, kv_col)),
        ],
        out_specs=pl.BlockSpec((BLOCK, dq), lambda b, n: (b * nb + n, 0)),
        compiler_params=_params("parallel", "arbitrary"),
        name="swa_attention",
    )(sinks, qkv, qkv, qkv)


def _swa_weight(w_qkv):
    dq = SWA_HEADS * SWA_HEAD_DIM
    hd = SWA_HEAD_DIM
    wq = w_qkv[:, :dq]
    wk = w_qkv[:, dq:dq + SWA_KV_HEADS * hd].reshape(D_MODEL, SWA_KV_HEADS, 1, hd)
    wv = w_qkv[:, dq + SWA_KV_HEADS * hd:].reshape(D_MODEL, SWA_KV_HEADS, 1, hd)
    dup = lambda w: jnp.broadcast_to(w, (D_MODEL, SWA_KV_HEADS, 2, hd)).reshape(D_MODEL, 2 * SWA_KV_HEADS * hd)
    return jnp.concatenate([wq, dup(wk), dup(wv)], axis=1)


def _sb_kernel(q_ref, k_ref, v_ref, o_ref, kx_ref, vx_ref, run_ref, acc_ref):
    n = pl.program_id(2)
    nkb = k_ref.shape[0] // BLOCK
    sub = SB_TILE // BLOCK
    low = lax.broadcasted_iota(jnp.int32, (nkb, BLOCK, LANES), 2) < SB_HEAD_DIM

    @pl.when(n == 0)
    def _():
        for src, dst in ((k_ref, kx_ref), (v_ref, vx_ref)):
            x = src[...].reshape(nkb, BLOCK, LANES)
            zero = jnp.zeros_like(x)
            dst[:, :BLOCK, :] = jnp.where(low, x, zero)
            dst[:, BLOCK:, :] = jnp.where(low, zero, x)

    q = q_ref[...] * jnp.asarray(SB_HEAD_DIM ** -0.5, BF16)
    tj = lax.broadcasted_iota(jnp.int32, (2 * BLOCK, 2 * BLOCK), 0) % BLOCK
    ts = lax.broadcasted_iota(jnp.int32, (2 * BLOCK, 2 * BLOCK), 1)
    tri = jnp.where((tj > ts) | (ts >= BLOCK), 1.0, 0.0).astype(BF16)
    ti = lax.broadcasted_iota(jnp.int32, (BLOCK, 2 * BLOCK), 0)
    si = lax.broadcasted_iota(jnp.int32, (BLOCK, 2 * BLOCK), 1) % BLOCK
    causal = si < ti
    run_ref[...] = jnp.zeros_like(run_ref)
    acc_ref[...] = jnp.zeros_like(acc_ref)

    def mask_first_block(x):
        head = jnp.where(causal, x[:BLOCK], 0.0)
        return head if x.shape[0] == BLOCK else jnp.concatenate([head, x[BLOCK:]], axis=0)

    def slab(kb, r0, diag):
        z = lax.dot_general(q[r0:], kx_ref[kb], (((1,), (1,)), ((), ())), preferred_element_type=F32)
        t = jnp.maximum(z, 0.0) + jnp.log(1.0 + jnp.exp2(jnp.abs(z) * -LOG2E))
        log_beta = z - t
        if diag:
            t = mask_first_block(t)
        hi = t.astype(BF16)
        lo = (t - hi.astype(F32)).astype(BF16)
        c = [jnp.dot(jnp.concatenate([hi[:, s:s + BLOCK], lo[:, s:s + BLOCK]], axis=1), tri,
                     preferred_element_type=F32) for s in (0, BLOCK)]
        cum = jnp.concatenate([c[0][:, :BLOCK], c[1][:, :BLOCK]], axis=1)
        tot = jnp.concatenate([c[0][:, BLOCK:], c[1][:, BLOCK:]], axis=1)
        run = run_ref[r0:, :]
        a = jnp.exp(log_beta - cum - run)
        if diag:
            a = mask_first_block(a)
        acc_ref[r0:, :] += jnp.dot(a.astype(BF16), vx_ref[kb], preferred_element_type=F32)
        run_ref[r0:, :] = run + tot

    for j in reversed(range(sub)):
        slab(n * sub + j, j * BLOCK, True)

    def tile(carry):
        i, _ = carry
        kb0 = (n - 1 - i) * sub
        for j in reversed(range(sub)):
            slab(kb0 + j, 0, False)
        return i + 1, jnp.min(run_ref[...])

    lax.while_loop(lambda c: (c[0] < n) & (c[1] < SB_EXP_ZERO), tile, (jnp.int32(0), jnp.float32(0.0)))
    o_ref[...] = acc_ref[...].astype(o_ref.dtype)


def sb_attention(qkv, *, batch, seq):
    nt = seq // SB_TILE
    nkb = seq // BLOCK
    pairs = SB_HEADS * SB_HEAD_DIM // LANES
    return pl.pallas_call(
        _sb_kernel,
        out_shape=jax.ShapeDtypeStruct((batch * seq, SB_HEADS * SB_HEAD_DIM), BF16),
        grid=(batch, pairs, nt),
        in_specs=[
            pl.BlockSpec((SB_TILE, LANES), lambda b, p, n: (b * nt + n, p)),
            pl.BlockSpec((seq, LANES), lambda b, p, n: (b, pairs + p)),
            pl.BlockSpec((seq, LANES), lambda b, p, n: (b, 2 * pairs + p)),
        ],
        out_specs=pl.BlockSpec((SB_TILE, LANES), lambda b, p, n: (b * nt + n, p)),
        scratch_shapes=[
            pltpu.VMEM((nkb, 2 * BLOCK, LANES), BF16),
            pltpu.VMEM((nkb, 2 * BLOCK, LANES), BF16),
            pltpu.VMEM((SB_TILE, 2 * LANES), F32),
            pltpu.VMEM((SB_TILE, LANES), F32),
        ],
        compiler_params=_params("parallel", "parallel", "arbitrary"),
        name="sb_attention",
    )(qkv, qkv, qkv)


def _ret_kernel(cd_ref, qkv_ref, g_ref, dm_ref, qd_ref, kd_ref, y_ref, state_ref):
    hh, dk, dv = RET_HEADS, RET_QK_DIM, RET_V_DIM

    @pl.when(pl.program_id(1) == 0)
    def _():
        state_ref[...] = jnp.zeros_like(state_ref)

    for h in range(hh):
        q = qkv_ref[:, h * dk:(h + 1) * dk]
        kf = qkv_ref[:, (hh + h) * dk:(hh + h + 1) * dk].astype(F32) * (dk ** -0.5)
        v = qkv_ref[:, 2 * hh * dk + h * dv:2 * hh * dk + (h + 1) * dv]
        scores = lax.dot_general(q, kf.astype(BF16), (((1,), (1,)), ((), ())), preferred_element_type=F32) * dm_ref[h]
        state = state_ref[h]
        qd = (q.astype(F32) * qd_ref[h]).astype(BF16)
        o = (jnp.dot(scores.astype(BF16), v, preferred_element_type=F32)
             + jnp.dot(qd, state.astype(BF16), preferred_element_type=F32))
        kd = (kf * kd_ref[h]).astype(BF16)
        state_ref[h] = state * cd_ref[h] + lax.dot_general(kd, v, (((0,), (0,)), ((), ())),
                                                           preferred_element_type=F32)
        o = o * lax.rsqrt(jnp.mean(o * o, axis=-1, keepdims=True) + RMS_EPS)
        gate = g_ref[:, h * dv:(h + 1) * dv]
        y_ref[:, h * dv:(h + 1) * dv] = (gate * (1.0 / (1.0 + jnp.exp(-gate))) * o).astype(y_ref.dtype)


def _ret_decay_tables():
    c, h = BLOCK, RET_HEADS
    log_gamma = jnp.log1p(-jnp.exp2(-5.0 - jnp.arange(h, dtype=F32)))
    idx = jnp.arange(c)
    diff = idx[:, None] - idx[None, :]
    decay_mat = jnp.where(diff >= 0, jnp.exp(log_gamma[:, None, None] * jnp.maximum(diff, 0).astype(F32)), 0.0)
    q_decay = jnp.exp(log_gamma[:, None] * (idx + 1).astype(F32))[:, :, None]
    k_decay = jnp.exp(log_gamma[:, None] * (c - 1 - idx).astype(F32))[:, :, None]
    chunk_decay = jnp.exp(log_gamma * c)
    return decay_mat, q_decay, k_decay, chunk_decay


def retention(qkv, gate, *, batch, seq):
    nc = seq // BLOCK
    hh, dk, dv = RET_HEADS, RET_QK_DIM, RET_V_DIM
    dm, qd, kd, cd = _ret_decay_tables()
    return pl.pallas_call(
        _ret_kernel,
        out_shape=jax.ShapeDtypeStruct((batch * seq, hh * dv), BF16),
        grid=(batch, nc),
        in_specs=[
            pl.BlockSpec(memory_space=pltpu.SMEM),
            pl.BlockSpec((BLOCK, qkv.shape[1]), lambda b, n: (b * nc + n, 0)),
            pl.BlockSpec((BLOCK, hh * dv), lambda b, n: (b * nc + n, 0)),
            _resident(dm.shape),
            _resident(qd.shape),
            _resident(kd.shape),
        ],
        out_specs=pl.BlockSpec((BLOCK, hh * dv), lambda b, n: (b * nc + n, 0)),
        scratch_shapes=[pltpu.VMEM((hh, dk, dv), F32)],
        compiler_params=_params("parallel", "arbitrary"),
        name="retention",
    )(cd, qkv, gate, dm, qd, kd)


def kernel(x, attn_norm, mlp_norm, final_norm, swa_w_qkv, swa_sinks, swa_w_o, sb_w_qkv, sb_w_o, ret_w_in, ret_w_o,
           mlp_w_up, mlp_w_down):
    batch, seq, d = x.shape
    h = x.reshape(batch * seq, d)
    ret_qkv_dim = 2 * RET_HEADS * RET_QK_DIM + RET_HEADS * RET_V_DIM
    for i in range(DEPTH):
        kind, j = i % N_MIXERS, i // N_MIXERS
        if kind == 0:
            qkv, = norm_proj(h, attn_norm[i], [_swa_weight(swa_w_qkv[j]).astype(BF16)], [BF16])
            mix = swa_attention(qkv, swa_sinks[j], batch=batch, seq=seq)
            w_o = swa_w_o[j]
        elif kind == 1:
            qkv, = norm_proj(h, attn_norm[i], [sb_w_qkv[j].astype(BF16)], [BF16])
            mix = sb_attention(qkv, batch=batch, seq=seq)
            w_o = sb_w_o[j]
        else:
            w_in = ret_w_in[j].astype(BF16)
            qkv, gate = norm_proj(h, attn_norm[i], [w_in[:, :ret_qkv_dim], w_in[:, ret_qkv_dim:]], [BF16, F32])
            mix = retention(qkv, gate, batch=batch, seq=seq)
            w_o = ret_w_o[j]
        h = layer_tail(mix, w_o.astype(BF16), h, mlp_norm[i], mlp_w_up[i].astype(BF16), mlp_w_down[i].astype(BF16),
                       final_gain=final_norm if i == DEPTH - 1 else None)
    return h.reshape(batch, seq, d)
```

```python
import functools

import jax
import jax.numpy as jnp
from jax import lax
from jax.experimental import pallas as pl
from jax.experimental.pallas import tpu as pltpu

D_MODEL = 1024
DEPTH = 4
N_MIXERS = 3
RMS_EPS = 1e-6
BLOCK = 128
NEG = -1e30

SWA_HEAD_DIM = 64
SWA_HEADS = 16
SWA_KV_HEADS = 2
SWA_GROUP = 8
SB_HEAD_DIM = 64
SB_HEADS = 16
SB_TILE = 256
SB_PAIRS = 4
SB_EXP_ZERO = 128.0
RET_QK_DIM = 256
RET_HEADS = 4
RET_V_DIM = 512
D_FF = 4 * D_MODEL

LANES = 128
VMEM_LIMIT = 48 * 1024 * 1024
TOKEN_TILE = 512
COL_CHUNK = 1024
FF_TILE = 2048
LOG2E = 1.4426950408889634

F32 = jnp.float32
BF16 = jnp.bfloat16


def _params(*sem):
    return pltpu.CompilerParams(dimension_semantics=sem, vmem_limit_bytes=VMEM_LIMIT)


def _rms(x, g):
    return x * lax.rsqrt(jnp.mean(x * x, axis=-1, keepdims=True) + RMS_EPS) * g


def _resident(shape):
    return pl.BlockSpec(shape, lambda *_: (0,) * len(shape), pipeline_mode=pl.Buffered(1))


def _norm_proj_kernel(n_out, x_ref, g_ref, *refs):
    w_refs, o_refs = refs[:n_out], refs[n_out:]
    xn = _rms(x_ref[...], g_ref[...]).astype(BF16)
    for w_ref, o_ref in zip(w_refs, o_refs):
        for c in range(0, w_ref.shape[1], COL_CHUNK):
            ce = min(c + COL_CHUNK, w_ref.shape[1])
            o_ref[:, c:ce] = jnp.dot(xn, w_ref[:, c:ce], preferred_element_type=F32).astype(o_ref.dtype)


def norm_proj(x, g, ws, dtypes):
    n, d = x.shape
    tm = TOKEN_TILE
    return pl.pallas_call(
        functools.partial(_norm_proj_kernel, len(ws)),
        out_shape=[jax.ShapeDtypeStruct((n, w.shape[1]), dt) for w, dt in zip(ws, dtypes)],
        grid=(n // tm,),
        in_specs=[pl.BlockSpec((tm, d), lambda i: (i, 0)), _resident((1, d))] + [_resident(w.shape) for w in ws],
        out_specs=[pl.BlockSpec((tm, w.shape[1]), lambda i: (i, 0)) for w in ws],
        compiler_params=_params("parallel"),
        name="norm_proj",
    )(x, g.reshape(1, d), *ws)


def _tail_kernel(final, mix_ref, wo_ref, h_ref, g_ref, wu_ref, wd_ref, *refs):
    if final:
        fg_ref, o_ref, xn_ref = refs
    else:
        o_ref, xn_ref = refs
    j = pl.program_id(1)

    @pl.when(j == 0)
    def _():
        h1 = h_ref[...] + jnp.dot(mix_ref[...], wo_ref[...], preferred_element_type=F32)
        xn_ref[...] = _rms(h1, g_ref[...]).astype(BF16)
        o_ref[...] = h1

    a = jnp.maximum(jnp.dot(xn_ref[...], wu_ref[...], preferred_element_type=F32), 0.0)
    o_ref[...] += jnp.dot((a * a).astype(BF16), wd_ref[...], preferred_element_type=F32)

    if final:
        @pl.when(j == pl.num_programs(1) - 1)
        def _():
            o_ref[...] = _rms(o_ref[...], fg_ref[...])


def layer_tail(mix, w_o, h, g, w_up, w_down, final_gain=None):
    n, d = h.shape
    k = mix.shape[1]
    dff = w_up.shape[1]
    tm, tf = TOKEN_TILE, FF_TILE
    final = final_gain is not None
    in_specs = [
        pl.BlockSpec((tm, k), lambda i, j: (i, 0)),
        _resident((k, d)),
        pl.BlockSpec((tm, d), lambda i, j: (i, 0)),
        _resident((1, d)),
        pl.BlockSpec((d, tf), lambda i, j: (0, j)),
        pl.BlockSpec((tf, d), lambda i, j: (j, 0)),
    ]
    args = [mix, w_o, h, g.reshape(1, d), w_up, w_down]
    if final:
        in_specs.append(_resident((1, d)))
        args.append(final_gain.reshape(1, d))
    return pl.pallas_call(
        functools.partial(_tail_kernel, final),
        out_shape=jax.ShapeDtypeStruct((n, d), F32),
        grid=(n // tm, dff // tf),
        in_specs=in_specs,
        out_specs=pl.BlockSpec((tm, d), lambda i, j: (i, 0)),
        scratch_shapes=[pltpu.VMEM((tm, d), BF16)],
        compiler_params=_params("parallel", "arbitrary"),
        name="layer_tail",
    )(*args)


def _swa_kernel(sink_ref, q_ref, kvp_ref, kvc_ref, o_ref):
    n = pl.program_id(1)
    q = q_ref[...] * jnp.asarray(SWA_HEAD_DIM ** -0.5, BF16)
    kv = jnp.concatenate([kvp_ref[...], kvc_ref[...]], axis=0)
    qi = lax.broadcasted_iota(jnp.int32, (BLOCK, BLOCK), 0)
    ci = lax.broadcasted_iota(jnp.int32, (BLOCK, BLOCK), 1)
    upper = ci > qi
    valid = jnp.logical_not(upper) | (n > 0)
    distf = jnp.where(upper, BLOCK + qi - ci, qi - ci).astype(F32)
    low = lax.broadcasted_iota(jnp.int32, (BLOCK, LANES), 1) < SWA_HEAD_DIM
    for g in range(SWA_KV_HEADS):
        kk = kv[:, g * LANES:(g + 1) * LANES]
        vv = kv[:, (SWA_KV_HEADS + g) * LANES:(SWA_KV_HEADS + g + 1) * LANES]
        for p in range(SWA_GROUP // 2):
            h0 = g * SWA_GROUP + 2 * p
            q2 = q[:, h0 * SWA_HEAD_DIM:h0 * SWA_HEAD_DIM + LANES]
            outs = []
            for half in range(2):
                h = h0 + half
                slope = 2.0 ** (-8.0 * (h + 1) / SWA_HEADS)
                qm = jnp.where(low if half == 0 else jnp.logical_not(low), q2, jnp.zeros_like(q2))
                s = lax.dot_general(qm, kk, (((1,), (1,)), ((), ())), preferred_element_type=F32)
                s = jnp.where(upper, s[:, :BLOCK], s[:, BLOCK:])
                logits = jnp.where(valid, s - slope * distf, NEG)
                sink = sink_ref[h]
                m = jnp.maximum(jnp.max(logits, axis=-1, keepdims=True), sink)
                pexp = jnp.exp(logits - m)
                inv = 1.0 / (jnp.sum(pexp, axis=-1, keepdims=True) + jnp.exp(sink - m))
                pb = pexp.astype(BF16)
                zero = jnp.zeros_like(pb)
                p2 = jnp.concatenate([jnp.where(upper, pb, zero), jnp.where(upper, zero, pb)], axis=1)
                outs.append(jnp.dot(p2, vv, preferred_element_type=F32) * inv)
            o_ref[:, h0 * SWA_HEAD_DIM:h0 * SWA_HEAD_DIM + LANES] = jnp.where(low, outs[0], outs[1]).astype(o_ref.dtype)


def swa_attention(qkv, sinks, *, batch, seq):
    nb = seq // BLOCK
    dq = SWA_HEADS * SWA_HEAD_DIM
    dkv = 2 * SWA_KV_HEADS * LANES
    kv_col = dq // dkv
    return pl.pallas_call(
        _swa_kernel,
        out_shape=jax.ShapeDtypeStruct((batch * seq, dq), BF16),
        grid=(batch, nb),
        in_specs=[
            pl.BlockSpec(memory_space=pltpu.SMEM),
            pl.BlockSpec((BLOCK, dq), lambda b, n: (b * nb + n, 0)),
            pl.BlockSpec((BLOCK, dkv), lambda b, n: (b * nb + jnp.maximum(n - 1, 0), kv_col)),
            pl.BlockSpec((BLOCK, dkv), lambda b, n: (b * nb + n, kv_col)),
        ],
        out_specs=pl.BlockSpec((BLOCK, dq), lambda b, n: (b * nb + n, 0)),
        compiler_params=_params("parallel", "arbitrary"),
        name="swa_attention",
    )(sinks, qkv, qkv, qkv)


def _swa_weight(w_qkv):
    dq = SWA_HEADS * SWA_HEAD_DIM
    hd = SWA_HEAD_DIM
    wq = w_qkv[:, :dq]
    wk = w_qkv[:, dq:dq + SWA_KV_HEADS * hd].reshape(D_MODEL, SWA_KV_HEADS, 1, hd)
    wv = w_qkv[:, dq + SWA_KV_HEADS * hd:].reshape(D_MODEL, SWA_KV_HEADS, 1, hd)
    dup = lambda w: jnp.broadcast_to(w, (D_MODEL, SWA_KV_HEADS, 2, hd)).reshape(D_MODEL, 2 * SWA_KV_HEADS * hd)
    return jnp.concatenate([wq, dup(wk), dup(wv)], axis=1)


def _sb_kernel(q_ref, k_ref, v_ref, o_ref, kx_ref, vx_ref, run_ref, acc_ref):
    n = pl.program_id(2)
    nkb = k_ref.shape[0] // BLOCK
    sub = SB_TILE // BLOCK
    low = lax.broadcasted_iota(jnp.int32, (nkb, BLOCK, LANES), 2) < SB_HEAD_DIM

    @pl.when(n == 0)
    def _():
        for src, dst in ((k_ref, kx_ref), (v_ref, vx_ref)):
            for pp in range(SB_PAIRS):
                x = src[:, pp * LANES:(pp + 1) * LANES].reshape(nkb, BLOCK, LANES)
                zero = jnp.zeros_like(x)
                dst[pp, :, :BLOCK, :] = jnp.where(low, x, zero)
                dst[pp, :, BLOCK:, :] = jnp.where(low, zero, x)

    q = q_ref[...] * jnp.asarray(SB_HEAD_DIM ** -0.5, BF16)
    tj = lax.broadcasted_iota(jnp.int32, (2 * BLOCK, 2 * BLOCK), 0) % BLOCK
    ts = lax.broadcasted_iota(jnp.int32, (2 * BLOCK, 2 * BLOCK), 1)
    tri = jnp.where((tj > ts) | (ts >= BLOCK), 1.0, 0.0).astype(BF16)
    ti = lax.broadcasted_iota(jnp.int32, (BLOCK, 2 * BLOCK), 0)
    si = lax.broadcasted_iota(jnp.int32, (BLOCK, 2 * BLOCK), 1) % BLOCK
    causal = si < ti
    run_ref[...] = jnp.zeros_like(run_ref)
    acc_ref[...] = jnp.zeros_like(acc_ref)

    def mask_first_block(x):
        head = jnp.where(causal, x[:BLOCK], 0.0)
        return head if x.shape[0] == BLOCK else jnp.concatenate([head, x[BLOCK:]], axis=0)

    def slabs(specs, diag):
        chains = [(pp, kb, r0) for kb, r0 in specs for pp in range(SB_PAIRS)]
        cols = lambda pp: slice(pp * LANES, (pp + 1) * LANES)
        state = [None] * len(chains)

        def scores(i):
            pp, kb, r0 = chains[i]
            state[i] = lax.dot_general(q[r0:, cols(pp)], kx_ref[pp, kb], (((1,), (1,)), ((), ())),
                                       preferred_element_type=F32)

        def prefix(i):
            z = state[i]
            t = jnp.maximum(z, 0.0) + jnp.log(1.0 + jnp.exp2(jnp.abs(z) * -LOG2E))
            log_beta = z - t
            if diag:
                t = mask_first_block(t)
            hi = t.astype(BF16)
            lo = (t - hi.astype(F32)).astype(BF16)
            c = [jnp.dot(jnp.concatenate([hi[:, s:s + BLOCK], lo[:, s:s + BLOCK]], axis=1), tri,
                         preferred_element_type=F32) for s in (0, BLOCK)]
            state[i] = (log_beta, c)

        def values(i):
            pp, kb, r0 = chains[i]
            log_beta, c = state[i]
            cum = jnp.concatenate([c[0][:, :BLOCK], c[1][:, :BLOCK]], axis=1)
            tot = jnp.concatenate([c[0][:, BLOCK:], c[1][:, BLOCK:]], axis=1)
            run = run_ref[pp, r0:, :]
            a = jnp.exp(log_beta - cum - run)
            if diag:
                a = mask_first_block(a)
            acc_ref[r0:, cols(pp)] += jnp.dot(a.astype(BF16), vx_ref[pp, kb], preferred_element_type=F32)
            run_ref[pp, r0:, :] = run + tot
            state[i] = None

        stages = (scores, prefix, values)
        for tick in range(len(chains) + len(stages) - 1):
            for depth, stage in enumerate(stages):
                if 0 <= tick - depth < len(chains):
                    stage(tick - depth)

    slabs([(n * sub + j, j * BLOCK) for j in reversed(range(sub))], True)

    def tile(carry):
        i, _ = carry
        kb0 = (n - 1 - i) * sub
        slabs([(kb0 + j, 0) for j in reversed(range(sub))], False)
        return i + 1, jnp.min(run_ref[...])

    lax.while_loop(lambda c: (c[0] < n) & (c[1] < SB_EXP_ZERO), tile, (jnp.int32(0), jnp.float32(0.0)))
    o_ref[...] = acc_ref[...].astype(o_ref.dtype)


def sb_attention(qkv, *, batch, seq):
    nt = seq // SB_TILE
    nkb = seq // BLOCK
    width = SB_PAIRS * LANES
    groups = SB_HEADS * SB_HEAD_DIM // width
    return pl.pallas_call(
        _sb_kernel,
        out_shape=jax.ShapeDtypeStruct((batch * seq, SB_HEADS * SB_HEAD_DIM), BF16),
        grid=(batch, groups, nt),
        in_specs=[
            pl.BlockSpec((SB_TILE, width), lambda b, p, n: (b * nt + n, p)),
            pl.BlockSpec((seq, width), lambda b, p, n: (b, groups + p)),
            pl.BlockSpec((seq, width), lambda b, p, n: (b, 2 * groups + p)),
        ],
        out_specs=pl.BlockSpec((SB_TILE, width), lambda b, p, n: (b * nt + n, p)),
        scratch_shapes=[
            pltpu.VMEM((SB_PAIRS, nkb, 2 * BLOCK, LANES), BF16),
            pltpu.VMEM((SB_PAIRS, nkb, 2 * BLOCK, LANES), BF16),
            pltpu.VMEM((SB_PAIRS, SB_TILE, 2 * LANES), F32),
            pltpu.VMEM((SB_TILE, width), F32),
        ],
        compiler_params=_params("parallel", "parallel", "arbitrary"),
        name="sb_attention",
    )(qkv, qkv, qkv)


def _ret_kernel(cd_ref, qkv_ref, g_ref, dm_ref, qd_ref, kd_ref, y_ref, state_ref):
    hh, dk, dv = RET_HEADS, RET_QK_DIM, RET_V_DIM

    @pl.when(pl.program_id(1) == 0)
    def _():
        state_ref[...] = jnp.zeros_like(state_ref)

    for h in range(hh):
        q = qkv_ref[:, h * dk:(h + 1) * dk]
        kf = qkv_ref[:, (hh + h) * dk:(hh + h + 1) * dk].astype(F32) * (dk ** -0.5)
        v = qkv_ref[:, 2 * hh * dk + h * dv:2 * hh * dk + (h + 1) * dv]
        scores = lax.dot_general(q, kf.astype(BF16), (((1,), (1,)), ((), ())), preferred_element_type=F32) * dm_ref[h]
        state = state_ref[h]
        qd = (q.astype(F32) * qd_ref[h]).astype(BF16)
        o = (jnp.dot(scores.astype(BF16), v, preferred_element_type=F32)
             + jnp.dot(qd, state.astype(BF16), preferred_element_type=F32))
        kd = (kf * kd_ref[h]).astype(BF16)
        state_ref[h] = state * cd_ref[h] + lax.dot_general(kd, v, (((0,), (0,)), ((), ())),
                                                           preferred_element_type=F32)
        o = o * lax.rsqrt(jnp.mean(o * o, axis=-1, keepdims=True) + RMS_EPS)
        gate = g_ref[:, h * dv:(h + 1) * dv]
        y_ref[:, h * dv:(h + 1) * dv] = (gate * (1.0 / (1.0 + jnp.exp(-gate))) * o).astype(y_ref.dtype)


def _ret_decay_tables():
    c, h = BLOCK, RET_HEADS
    log_gamma = jnp.log1p(-jnp.exp2(-5.0 - jnp.arange(h, dtype=F32)))
    idx = jnp.arange(c)
    diff = idx[:, None] - idx[None, :]
    decay_mat = jnp.where(diff >= 0, jnp.exp(log_gamma[:, None, None] * jnp.maximum(diff, 0).astype(F32)), 0.0)
    q_decay = jnp.exp(log_gamma[:, None] * (idx + 1).astype(F32))[:, :, None]
    k_decay = jnp.exp(log_gamma[:, None] * (c - 1 - idx).astype(F32))[:, :, None]
    chunk_decay = jnp.exp(log_gamma * c)
    return decay_mat, q_decay, k_decay, chunk_decay


def retention(qkv, gate, *, batch, seq):
    nc = seq // BLOCK
    hh, dk, dv = RET_HEADS, RET_QK_DIM, RET_V_DIM
    dm, qd, kd, cd = _ret_decay_tables()
    return pl.pallas_call(
        _ret_kernel,
        out_shape=jax.ShapeDtypeStruct((batch * seq, hh * dv), BF16),
        grid=(batch, nc),
        in_specs=[
            pl.BlockSpec(memory_space=pltpu.SMEM),
            pl.BlockSpec((BLOCK, qkv.shape[1]), lambda b, n: (b * nc + n, 0)),
            pl.BlockSpec((BLOCK, hh * dv), lambda b, n: (b * nc + n, 0)),
            _resident(dm.shape),
            _resident(qd.shape),
            _resident(kd.shape),
        ],
        out_specs=pl.BlockSpec((BLOCK, hh * dv), lambda b, n: (b * nc + n, 0)),
        scratch_shapes=[pltpu.VMEM((hh, dk, dv), F32)],
        compiler_params=_params("parallel", "arbitrary"),
        name="retention",
    )(cd, qkv, gate, dm, qd, kd)


def kernel(x, attn_norm, mlp_norm, final_norm, swa_w_qkv, swa_sinks, swa_w_o, sb_w_qkv, sb_w_o, ret_w_in, ret_w_o,
           mlp_w_up, mlp_w_down):
    batch, seq, d = x.shape
    h = x.reshape(batch * seq, d)
    ret_qkv_dim = 2 * RET_HEADS * RET_QK_DIM + RET_HEADS * RET_V_DIM
    for i in range(DEPTH):
        kind, j = i % N_MIXERS, i // N_MIXERS
        if kind == 0:
            qkv, = norm_proj(h, attn_norm[i], [_swa_weight(swa_w_qkv[j]).astype(BF16)], [BF16])
            mix = swa_attention(qkv, swa_sinks[j], batch=batch, seq=seq)
            w_o = swa_w_o[j]
        elif kind == 1:
            qkv, = norm_proj(h, attn_norm[i], [sb_w_qkv[j].astype(BF16)], [BF16])
            mix = sb_attention(qkv, batch=batch, seq=seq)
            w_o = sb_w_o[j]
        else:
            w_in = ret_w_in[j].astype(BF16)
            qkv, gate = norm_proj(h, attn_norm[i], [w_in[:, :ret_qkv_dim], w_in[:, ret_qkv_dim:]], [BF16, F32])
            mix = retention(qkv, gate, batch=batch, seq=seq)
            w_o = ret_w_o[j]
        h = layer_tail(mix, w_o.astype(BF16), h, mlp_norm[i], mlp_w_up[i].astype(BF16), mlp_w_down[i].astype(BF16),
                       final_gain=final_norm if i == DEPTH - 1 else None)
    return h.reshape(batch, seq, d)
```

```python
import functools

import jax
import jax.numpy as jnp
from jax import lax
from jax.experimental import pallas as pl
from jax.experimental.pallas import tpu as pltpu

D_MODEL = 1024
DEPTH = 4
N_MIXERS = 3
RMS_EPS = 1e-6
BLOCK = 128
NEG = -1e30

SWA_HEAD_DIM = 64
SWA_HEADS = 16
SWA_KV_HEADS = 2
SWA_GROUP = 8
SB_HEAD_DIM = 64
SB_HEADS = 16
SB_TILE = 256
SB_PAIRS = 4
SB_EXP_ZERO = 104.0
RET_QK_DIM = 256
RET_HEADS = 4
RET_V_DIM = 512
D_FF = 4 * D_MODEL

LANES = 128
VMEM_LIMIT = 48 * 1024 * 1024
TOKEN_TILE = 512
COL_CHUNK = 1024
FF_TILE = 2048
LOG2E = 1.4426950408889634

F32 = jnp.float32
BF16 = jnp.bfloat16


def _params(*sem):
    return pltpu.CompilerParams(dimension_semantics=sem, vmem_limit_bytes=VMEM_LIMIT)


def _rms(x, g):
    return x * lax.rsqrt(jnp.mean(x * x, axis=-1, keepdims=True) + RMS_EPS) * g


def _staggered(stages, n):
    for tick in range(n + len(stages) - 1):
        for depth, stage in enumerate(stages):
            if 0 <= tick - depth < n:
                stage(tick - depth)


def _resident(shape):
    return pl.BlockSpec(shape, lambda *_: (0,) * len(shape), pipeline_mode=pl.Buffered(1))


def _norm_proj_kernel(n_out, x_ref, g_ref, *refs):
    w_refs, o_refs = refs[:n_out], refs[n_out:]
    xn = _rms(x_ref[...], g_ref[...]).astype(BF16)
    for w_ref, o_ref in zip(w_refs, o_refs):
        for c in range(0, w_ref.shape[1], COL_CHUNK):
            ce = min(c + COL_CHUNK, w_ref.shape[1])
            o_ref[:, c:ce] = jnp.dot(xn, w_ref[:, c:ce], preferred_element_type=F32).astype(o_ref.dtype)


def norm_proj(x, g, ws, dtypes):
    n, d = x.shape
    tm = TOKEN_TILE
    return pl.pallas_call(
        functools.partial(_norm_proj_kernel, len(ws)),
        out_shape=[jax.ShapeDtypeStruct((n, w.shape[1]), dt) for w, dt in zip(ws, dtypes)],
        grid=(n // tm,),
        in_specs=[pl.BlockSpec((tm, d), lambda i: (i, 0)), _resident((1, d))] + [_resident(w.shape) for w in ws],
        out_specs=[pl.BlockSpec((tm, w.shape[1]), lambda i: (i, 0)) for w in ws],
        compiler_params=_params("parallel"),
        name="norm_proj",
    )(x, g.reshape(1, d), *ws)


def _tail_kernel(final, mix_ref, wo_ref, h_ref, g_ref, wu_ref, wd_ref, *refs):
    if final:
        fg_ref, o_ref, xn_ref = refs
    else:
        o_ref, xn_ref = refs
    j = pl.program_id(1)

    @pl.when(j == 0)
    def _():
        h1 = h_ref[...] + jnp.dot(mix_ref[...], wo_ref[...], preferred_element_type=F32)
        xn_ref[...] = _rms(h1, g_ref[...]).astype(BF16)
        o_ref[...] = h1

    a = jnp.maximum(jnp.dot(xn_ref[...], wu_ref[...], preferred_element_type=F32), 0.0)
    o_ref[...] += jnp.dot((a * a).astype(BF16), wd_ref[...], preferred_element_type=F32)

    if final:
        @pl.when(j == pl.num_programs(1) - 1)
        def _():
            o_ref[...] = _rms(o_ref[...], fg_ref[...])


def layer_tail(mix, w_o, h, g, w_up, w_down, final_gain=None):
    n, d = h.shape
    k = mix.shape[1]
    dff = w_up.shape[1]
    tm, tf = TOKEN_TILE, FF_TILE
    final = final_gain is not None
    in_specs = [
        pl.BlockSpec((tm, k), lambda i, j: (i, 0)),
        _resident((k, d)),
        pl.BlockSpec((tm, d), lambda i, j: (i, 0)),
        _resident((1, d)),
        pl.BlockSpec((d, tf), lambda i, j: (0, j)),
        pl.BlockSpec((tf, d), lambda i, j: (j, 0)),
    ]
    args = [mix, w_o, h, g.reshape(1, d), w_up, w_down]
    if final:
        in_specs.append(_resident((1, d)))
        args.append(final_gain.reshape(1, d))
    return pl.pallas_call(
        functools.partial(_tail_kernel, final),
        out_shape=jax.ShapeDtypeStruct((n, d), F32),
        grid=(n // tm, dff // tf),
        in_specs=in_specs,
        out_specs=pl.BlockSpec((tm, d), lambda i, j: (i, 0)),
        scratch_shapes=[pltpu.VMEM((tm, d), BF16)],
        compiler_params=_params("parallel", "arbitrary"),
        name="layer_tail",
    )(*args)


def _swa_kernel(sink_ref, q_ref, kvp_ref, kvc_ref, o_ref):
    n = pl.program_id(1)
    q = q_ref[...] * jnp.asarray(SWA_HEAD_DIM ** -0.5, BF16)
    kv = jnp.concatenate([kvp_ref[...], kvc_ref[...]], axis=0)
    qi = lax.broadcasted_iota(jnp.int32, (BLOCK, BLOCK), 0)
    ci = lax.broadcasted_iota(jnp.int32, (BLOCK, BLOCK), 1)
    upper = ci > qi
    valid = jnp.logical_not(upper) | (n > 0)
    distf = jnp.where(upper, BLOCK + qi - ci, qi - ci).astype(F32)
    low = lax.broadcasted_iota(jnp.int32, (BLOCK, LANES), 1) < SWA_HEAD_DIM
    state = [None] * SWA_HEADS
    outs = [None] * SWA_HEADS

    def scores(h):
        g, c0 = h // SWA_GROUP, (h - h % 2) * SWA_HEAD_DIM
        q2 = q[:, c0:c0 + LANES]
        qm = jnp.where(low if h % 2 == 0 else jnp.logical_not(low), q2, jnp.zeros_like(q2))
        state[h] = lax.dot_general(qm, kv[:, g * LANES:(g + 1) * LANES], (((1,), (1,)), ((), ())),
                                   preferred_element_type=F32)

    def softmax(h):
        s = state[h]
        slope = 2.0 ** (-8.0 * (h + 1) / SWA_HEADS)
        s = jnp.where(upper, s[:, :BLOCK], s[:, BLOCK:])
        logits = jnp.where(valid, s - slope * distf, NEG)
        sink = sink_ref[h]
        m = jnp.maximum(jnp.max(logits, axis=-1, keepdims=True), sink)
        pexp = jnp.exp(logits - m)
        inv = 1.0 / (jnp.sum(pexp, axis=-1, keepdims=True) + jnp.exp(sink - m))
        pb = pexp.astype(BF16)
        zero = jnp.zeros_like(pb)
        state[h] = (jnp.concatenate([jnp.where(upper, pb, zero), jnp.where(upper, zero, pb)], axis=1), inv)

    def values(h):
        g, c0 = h // SWA_GROUP, (h - h % 2) * SWA_HEAD_DIM
        p2, inv = state[h]
        vv = kv[:, (SWA_KV_HEADS + g) * LANES:(SWA_KV_HEADS + g + 1) * LANES]
        outs[h] = jnp.dot(p2, vv, preferred_element_type=F32) * inv
        state[h] = None
        if h % 2 == 1:
            o_ref[:, c0:c0 + LANES] = jnp.where(low, outs[h - 1], outs[h]).astype(o_ref.dtype)
            outs[h - 1] = outs[h] = None

    _staggered((scores, softmax, values), SWA_HEADS)


def swa_attention(qkv, sinks, *, batch, seq):
    nb = seq // BLOCK
    dq = SWA_HEADS * SWA_HEAD_DIM
    dkv = 2 * SWA_KV_HEADS * LANES
    kv_col = dq // dkv
    return pl.pallas_call(
        _swa_kernel,
        out_shape=jax.ShapeDtypeStruct((batch * seq, dq), BF16),
        grid=(batch, nb),
        in_specs=[
            pl.BlockSpec(memory_space=pltpu.SMEM),
            pl.BlockSpec((BLOCK, dq), lambda b, n: (b * nb + n, 0)),
            pl.BlockSpec((BLOCK, dkv), lambda b, n: (b * nb + jnp.maximum(n - 1, 0), kv_col)),
            pl.BlockSpec((BLOCK, dkv), lambda b, n: (b * nb + n, kv_col)),
        ],
        out_specs=pl.BlockSpec((BLOCK, dq), lambda b, n: (b * nb + n, 0)),
        compiler_params=_params("parallel", "arbitrary"),
        name="swa_attention",
    )(sinks, qkv, qkv, qkv)


def _swa_weight(w_qkv):
    dq = SWA_HEADS * SWA_HEAD_DIM
    hd = SWA_HEAD_DIM
    wq = w_qkv[:, :dq]
    wk = w_qkv[:, dq:dq + SWA_KV_HEADS * hd].reshape(D_MODEL, SWA_KV_HEADS, 1, hd)
    wv = w_qkv[:, dq + SWA_KV_HEADS * hd:].reshape(D_MODEL, SWA_KV_HEADS, 1, hd)
    dup = lambda w: jnp.broadcast_to(w, (D_MODEL, SWA_KV_HEADS, 2, hd)).reshape(D_MODEL, 2 * SWA_KV_HEADS * hd)
    return jnp.concatenate([wq, dup(wk), dup(wv)], axis=1)


def _sb_kernel(q_ref, k_ref, v_ref, o_ref, kx_ref, vx_ref, run_ref, acc_ref):
    n = pl.program_id(2)
    nkb = k_ref.shape[0] // BLOCK
    sub = SB_TILE // BLOCK
    low = lax.broadcasted_iota(jnp.int32, (nkb, BLOCK, LANES), 2) < SB_HEAD_DIM

    @pl.when(n == 0)
    def _():
        for src, dst in ((k_ref, kx_ref), (v_ref, vx_ref)):
            for pp in range(SB_PAIRS):
                x = src[:, pp * LANES:(pp + 1) * LANES].reshape(nkb, BLOCK, LANES)
                zero = jnp.zeros_like(x)
                dst[pp, :, :BLOCK, :] = jnp.where(low, x, zero)
                dst[pp, :, BLOCK:, :] = jnp.where(low, zero, x)

    q = q_ref[...] * jnp.asarray(SB_HEAD_DIM ** -0.5, BF16)
    tj = lax.broadcasted_iota(jnp.int32, (2 * BLOCK, 2 * BLOCK), 0) % BLOCK
    ts = lax.broadcasted_iota(jnp.int32, (2 * BLOCK, 2 * BLOCK), 1)
    tri = jnp.where((tj > ts) | (ts >= BLOCK), 1.0, 0.0).astype(BF16)
    ti = lax.broadcasted_iota(jnp.int32, (BLOCK, 2 * BLOCK), 0)
    si = lax.broadcasted_iota(jnp.int32, (BLOCK, 2 * BLOCK), 1) % BLOCK
    causal = si < ti
    run_ref[...] = jnp.zeros_like(run_ref)
    acc_ref[...] = jnp.zeros_like(acc_ref)

    def mask_first_block(x):
        head = jnp.where(causal, x[:BLOCK], 0.0)
        return head if x.shape[0] == BLOCK else jnp.concatenate([head, x[BLOCK:]], axis=0)

    def slabs(specs, diag):
        chains = [(pp, kb, r0) for kb, r0 in specs for pp in range(SB_PAIRS)]
        cols = lambda pp: slice(pp * LANES, (pp + 1) * LANES)
        state = [None] * len(chains)

        def scores(i):
            pp, kb, r0 = chains[i]
            state[i] = lax.dot_general(q[r0:, cols(pp)], kx_ref[pp, kb], (((1,), (1,)), ((), ())),
                                       preferred_element_type=F32)

        def prefix(i):
            z = state[i]
            t = jnp.maximum(z, 0.0) + jnp.log(1.0 + jnp.exp2(jnp.abs(z) * -LOG2E))
            log_beta = z - t
            if diag:
                t = mask_first_block(t)
            hi = t.astype(BF16)
            lo = (t - hi.astype(F32)).astype(BF16)
            c = [jnp.dot(jnp.concatenate([hi[:, s:s + BLOCK], lo[:, s:s + BLOCK]], axis=1), tri,
                         preferred_element_type=F32) for s in (0, BLOCK)]
            state[i] = (log_beta, c)

        def values(i):
            pp, kb, r0 = chains[i]
            log_beta, c = state[i]
            cum = jnp.concatenate([c[0][:, :BLOCK], c[1][:, :BLOCK]], axis=1)
            tot = jnp.concatenate([c[0][:, BLOCK:], c[1][:, BLOCK:]], axis=1)
            run = run_ref[pp, r0:, :]
            a = jnp.exp(log_beta - cum - run)
            if diag:
                a = mask_first_block(a)
            acc_ref[r0:, cols(pp)] += jnp.dot(a.astype(BF16), vx_ref[pp, kb], preferred_element_type=F32)
            run_ref[pp, r0:, :] = run + tot
            state[i] = None

        _staggered((scores, prefix, values), len(chains))

    slabs([(n * sub + j, j * BLOCK) for j in reversed(range(sub))], True)

    def tile(carry):
        i, _ = carry
        kb0 = (n - 1 - i) * sub
        slabs([(kb0 + j, 0) for j in reversed(range(sub))], False)
        return i + 1, jnp.min(run_ref[...])

    lax.while_loop(lambda c: (c[0] < n) & (c[1] < SB_EXP_ZERO), tile, (jnp.int32(0), jnp.float32(0.0)))
    o_ref[...] = acc_ref[...].astype(o_ref.dtype)


def sb_attention(qkv, *, batch, seq):
    nt = seq // SB_TILE
    nkb = seq // BLOCK
    width = SB_PAIRS * LANES
    groups = SB_HEADS * SB_HEAD_DIM // width
    return pl.pallas_call(
        _sb_kernel,
        out_shape=jax.ShapeDtypeStruct((batch * seq, SB_HEADS * SB_HEAD_DIM), BF16),
        grid=(batch, groups, nt),
        in_specs=[
            pl.BlockSpec((SB_TILE, width), lambda b, p, n: (b * nt + n, p)),
            pl.BlockSpec((seq, width), lambda b, p, n: (b, groups + p)),
            pl.BlockSpec((seq, width), lambda b, p, n: (b, 2 * groups + p)),
        ],
        out_specs=pl.BlockSpec((SB_TILE, width), lambda b, p, n: (b * nt + n, p)),
        scratch_shapes=[
            pltpu.VMEM((SB_PAIRS, nkb, 2 * BLOCK, LANES), BF16),
            pltpu.VMEM((SB_PAIRS, nkb, 2 * BLOCK, LANES), BF16),
            pltpu.VMEM((SB_PAIRS, SB_TILE, 2 * LANES), F32),
            pltpu.VMEM((SB_TILE, width), F32),
        ],
        compiler_params=_params("parallel", "parallel", "arbitrary"),
        name="sb_attention",
    )(qkv, qkv, qkv)


def _ret_kernel(cd_ref, qkv_ref, g_ref, dm_ref, qd_ref, kd_ref, y_ref, state_ref):
    hh, dk, dv = RET_HEADS, RET_QK_DIM, RET_V_DIM

    @pl.when(pl.program_id(1) == 0)
    def _():
        state_ref[...] = jnp.zeros_like(state_ref)

    carry = [None] * hh
    value = lambda h: qkv_ref[:, 2 * hh * dk + h * dv:2 * hh * dk + (h + 1) * dv]

    def scores(h):
        q = qkv_ref[:, h * dk:(h + 1) * dk]
        kf = qkv_ref[:, (hh + h) * dk:(hh + h + 1) * dk].astype(F32) * (dk ** -0.5)
        s = lax.dot_general(q, kf.astype(BF16), (((1,), (1,)), ((), ())), preferred_element_type=F32) * dm_ref[h]
        qd = (q.astype(F32) * qd_ref[h]).astype(BF16)
        kd = (kf * kd_ref[h]).astype(BF16)
        carry[h] = (s.astype(BF16), qd, kd)

    def outputs(h):
        s, qd, kd = carry[h]
        v = value(h)
        state = state_ref[h]
        carry[h] = jnp.dot(s, v, preferred_element_type=F32) + jnp.dot(qd, state.astype(BF16),
                                                                      preferred_element_type=F32)
        state_ref[h] = state * cd_ref[h] + lax.dot_general(kd, v, (((0,), (0,)), ((), ())),
                                                           preferred_element_type=F32)

    def gated(h):
        o = carry[h]
        o = o * lax.rsqrt(jnp.mean(o * o, axis=-1, keepdims=True) + RMS_EPS)
        gate = g_ref[:, h * dv:(h + 1) * dv]
        y_ref[:, h * dv:(h + 1) * dv] = (gate * (1.0 / (1.0 + jnp.exp(-gate))) * o).astype(y_ref.dtype)
        carry[h] = None

    _staggered((scores, outputs, gated), hh)


def _ret_decay_tables():
    c, h = BLOCK, RET_HEADS
    log_gamma = jnp.log1p(-jnp.exp2(-5.0 - jnp.arange(h, dtype=F32)))
    idx = jnp.arange(c)
    diff = idx[:, None] - idx[None, :]
    decay_mat = jnp.where(diff >= 0, jnp.exp(log_gamma[:, None, None] * jnp.maximum(diff, 0).astype(F32)), 0.0)
    q_decay = jnp.exp(log_gamma[:, None] * (idx + 1).astype(F32))[:, :, None]
    k_decay = jnp.exp(log_gamma[:, None] * (c - 1 - idx).astype(F32))[:, :, None]
    chunk_decay = jnp.exp(log_gamma * c)
    return decay_mat, q_decay, k_decay, chunk_decay


def retention(qkv, gate, *, batch, seq):
    nc = seq // BLOCK
    hh, dk, dv = RET_HEADS, RET_QK_DIM, RET_V_DIM
    dm, qd, kd, cd = _ret_decay_tables()
    return pl.pallas_call(
        _ret_kernel,
        out_shape=jax.ShapeDtypeStruct((batch * seq, hh * dv), BF16),
        grid=(batch, nc),
        in_specs=[
            pl.BlockSpec(memory_space=pltpu.SMEM),
            pl.BlockSpec((BLOCK, qkv.shape[1]), lambda b, n: (b * nc + n, 0)),
            pl.BlockSpec((BLOCK, hh * dv), lambda b, n: (b * nc + n, 0)),
            _resident(dm.shape),
            _resident(qd.shape),
            _resident(kd.shape),
        ],
        out_specs=pl.BlockSpec((BLOCK, hh * dv), lambda b, n: (b * nc + n, 0)),
        scratch_shapes=[pltpu.VMEM((hh, dk, dv), F32)],
        compiler_params=_params("parallel", "arbitrary"),
        name="retention",
    )(cd, qkv, gate, dm, qd, kd)


def kernel(x, attn_norm, mlp_norm, final_norm, swa_w_qkv, swa_sinks, swa_w_o, sb_w_qkv, sb_w_o, ret_w_in, ret_w_o,
           mlp_w_up, mlp_w_down):
    batch, seq, d = x.shape
    h = x.reshape(batch * seq, d)
    ret_qkv_dim = 2 * RET_HEADS * RET_QK_DIM + RET_HEADS * RET_V_DIM
    for i in range(DEPTH):
        kind, j = i % N_MIXERS, i // N_MIXERS
        if kind == 0:
            qkv, = norm_proj(h, attn_norm[i], [_swa_weight(swa_w_qkv[j]).astype(BF16)], [BF16])
            mix = swa_attention(qkv, swa_sinks[j], batch=batch, seq=seq)
            w_o = swa_w_o[j]
        elif kind == 1:
            qkv, = norm_proj(h, attn_norm[i], [sb_w_qkv[j].astype(BF16)], [BF16])
            mix = sb_attention(qkv, batch=batch, seq=seq)
            w_o = sb_w_o[j]
        else:
            w_in = ret_w_in[j].astype(BF16)
            qkv, gate = norm_proj(h, attn_norm[i], [w_in[:, :ret_qkv_dim], w_in[:, ret_qkv_dim:]], [BF16, F32])
            mix = retention(qkv, gate, batch=batch, seq=seq)
            w_o = ret_w_o[j]
        h = layer_tail(mix, w_o.astype(BF16), h, mlp_norm[i], mlp_w_up[i].astype(BF16), mlp_w_down[i].astype(BF16),
                       final_gain=final_norm if i == DEPTH - 1 else None)
    return h.reshape(batch, seq, d)
```

```python
import functools

import jax
import jax.numpy as jnp
from jax import lax
from jax.experimental import pallas as pl
from jax.experimental.pallas import tpu as pltpu

D_MODEL = 1024
DEPTH = 4
N_MIXERS = 3
RMS_EPS = 1e-6
BLOCK = 128
NEG = -1e30

SWA_HEAD_DIM = 64
SWA_HEADS = 16
SWA_KV_HEADS = 2
SWA_GROUP = 8
SB_HEAD_DIM = 64
SB_HEADS = 16
SB_TILE = 256
SB_PAIRS = 4
SB_EXP_ZERO = 104.0
RET_QK_DIM = 256
RET_HEADS = 4
RET_V_DIM = 512
RET_STEP_CHUNKS = 4
D_FF = 4 * D_MODEL

LANES = 128
VMEM_LIMIT = 48 * 1024 * 1024
TOKEN_TILE = 512
COL_CHUNK = 1024
FF_TILE = 2048
LOG2E = 1.4426950408889634

F32 = jnp.float32
BF16 = jnp.bfloat16


def _params(*sem):
    return pltpu.CompilerParams(dimension_semantics=sem, vmem_limit_bytes=VMEM_LIMIT)


def _rms(x, g):
    return x * lax.rsqrt(jnp.mean(x * x, axis=-1, keepdims=True) + RMS_EPS) * g


def _staggered(stages, n):
    for tick in range(n + len(stages) - 1):
        for depth, stage in enumerate(stages):
            if 0 <= tick - depth < n:
                stage(tick - depth)


def _resident(shape):
    return pl.BlockSpec(shape, lambda *_: (0,) * len(shape), pipeline_mode=pl.Buffered(1))


def _norm_proj_kernel(outs, prepare, x_ref, g_ref, w_ref, *refs):
    o_refs = refs[:len(outs)]
    if prepare is not None:
        wb_ref = refs[len(outs)]

        @pl.when(pl.program_id(0) == 0)
        def _():
            prepare(w_ref, wb_ref)
    else:
        wb_ref = w_ref
    xn = _rms(x_ref[...], g_ref[...]).astype(BF16)
    for (c0, c1, _), o_ref in zip(outs, o_refs):
        for c in range(c0, c1, COL_CHUNK):
            ce = min(c + COL_CHUNK, c1)
            o_ref[:, c - c0:ce - c0] = jnp.dot(xn, wb_ref[:, c:ce], preferred_element_type=F32).astype(o_ref.dtype)


def norm_proj(x, g, w, outs, prepare=None, prepared_cols=None):
    n, d = x.shape
    tm = TOKEN_TILE
    scratch = [] if prepare is None else [pltpu.VMEM((d, prepared_cols), BF16)]
    return pl.pallas_call(
        functools.partial(_norm_proj_kernel, tuple(outs), prepare),
        out_shape=[jax.ShapeDtypeStruct((n, c1 - c0), dt) for c0, c1, dt in outs],
        grid=(n // tm,),
        in_specs=[pl.BlockSpec((tm, d), lambda i: (i, 0)), _resident((1, d)), _resident(w.shape)],
        out_specs=[pl.BlockSpec((tm, c1 - c0), lambda i: (i, 0)) for c0, c1, _ in outs],
        scratch_shapes=scratch,
        compiler_params=_params("parallel" if prepare is None else "arbitrary"),
        name="norm_proj",
    )(x, g.reshape(1, d), w)


def _cast_weight(w_ref, wb_ref):
    wb_ref[...] = w_ref[...].astype(BF16)


def _tail_kernel(final, mix_ref, wo_ref, h_ref, g_ref, wu_ref, wd_ref, *refs):
    if final:
        fg_ref, o_ref, xn_ref = refs
    else:
        o_ref, xn_ref = refs
    j = pl.program_id(1)

    @pl.when(j == 0)
    def _():
        h1 = h_ref[...] + jnp.dot(mix_ref[...], wo_ref[...], preferred_element_type=F32)
        xn_ref[...] = _rms(h1, g_ref[...]).astype(BF16)
        o_ref[...] = h1

    a = jnp.maximum(jnp.dot(xn_ref[...], wu_ref[...], preferred_element_type=F32), 0.0)
    o_ref[...] += jnp.dot((a * a).astype(BF16), wd_ref[...], preferred_element_type=F32)

    if final:
        @pl.when(j == pl.num_programs(1) - 1)
        def _():
            o_ref[...] = _rms(o_ref[...], fg_ref[...])


def layer_tail(mix, w_o, h, g, w_up, w_down, final_gain=None):
    n, d = h.shape
    k = mix.shape[1]
    dff = w_up.shape[1]
    tm, tf = TOKEN_TILE, FF_TILE
    final = final_gain is not None
    in_specs = [
        pl.BlockSpec((tm, k), lambda i, j: (i, 0)),
        _resident((k, d)),
        pl.BlockSpec((tm, d), lambda i, j: (i, 0)),
        _resident((1, d)),
        pl.BlockSpec((d, tf), lambda i, j: (0, j)),
        pl.BlockSpec((tf, d), lambda i, j: (j, 0)),
    ]
    args = [mix, w_o, h, g.reshape(1, d), w_up, w_down]
    if final:
        in_specs.append(_resident((1, d)))
        args.append(final_gain.reshape(1, d))
    return pl.pallas_call(
        functools.partial(_tail_kernel, final),
        out_shape=jax.ShapeDtypeStruct((n, d), F32),
        grid=(n // tm, dff // tf),
        in_specs=in_specs,
        out_specs=pl.BlockSpec((tm, d), lambda i, j: (i, 0)),
        scratch_shapes=[pltpu.VMEM((tm, d), BF16)],
        compiler_params=_params("parallel", "arbitrary"),
        name="layer_tail",
    )(*args)


def _swa_kernel(sink_ref, q_ref, kvp_ref, kvc_ref, o_ref):
    n = pl.program_id(1)
    q = q_ref[...] * jnp.asarray(SWA_HEAD_DIM ** -0.5, BF16)
    kv = jnp.concatenate([kvp_ref[...], kvc_ref[...]], axis=0)
    qi = lax.broadcasted_iota(jnp.int32, (BLOCK, BLOCK), 0)
    ci = lax.broadcasted_iota(jnp.int32, (BLOCK, BLOCK), 1)
    upper = ci > qi
    valid = jnp.logical_not(upper) | (n > 0)
    distf = jnp.where(upper, BLOCK + qi - ci, qi - ci).astype(F32)
    low = lax.broadcasted_iota(jnp.int32, (BLOCK, LANES), 1) < SWA_HEAD_DIM
    state = [None] * SWA_HEADS
    outs = [None] * SWA_HEADS

    def scores(h):
        g, c0 = h // SWA_GROUP, (h - h % 2) * SWA_HEAD_DIM
        q2 = q[:, c0:c0 + LANES]
        qm = jnp.where(low if h % 2 == 0 else jnp.logical_not(low), q2, jnp.zeros_like(q2))
        state[h] = lax.dot_general(qm, kv[:, g * LANES:(g + 1) * LANES], (((1,), (1,)), ((), ())),
                                   preferred_element_type=F32)

    def softmax(h):
        s = state[h]
        slope = 2.0 ** (-8.0 * (h + 1) / SWA_HEADS)
        s = jnp.where(upper, s[:, :BLOCK], s[:, BLOCK:])
        logits = jnp.where(valid, s - slope * distf, NEG)
        sink = sink_ref[h]
        m = jnp.maximum(jnp.max(logits, axis=-1, keepdims=True), sink)
        pexp = jnp.exp(logits - m)
        inv = 1.0 / (jnp.sum(pexp, axis=-1, keepdims=True) + jnp.exp(sink - m))
        pb = pexp.astype(BF16)
        zero = jnp.zeros_like(pb)
        state[h] = (jnp.concatenate([jnp.where(upper, pb, zero), jnp.where(upper, zero, pb)], axis=1), inv)

    def values(h):
        g, c0 = h // SWA_GROUP, (h - h % 2) * SWA_HEAD_DIM
        p2, inv = state[h]
        vv = kv[:, (SWA_KV_HEADS + g) * LANES:(SWA_KV_HEADS + g + 1) * LANES]
        outs[h] = jnp.dot(p2, vv, preferred_element_type=F32) * inv
        state[h] = None
        if h % 2 == 1:
            o_ref[:, c0:c0 + LANES] = jnp.where(low, outs[h - 1], outs[h]).astype(o_ref.dtype)
            outs[h - 1] = outs[h] = None

    _staggered((scores, softmax, values), SWA_HEADS)


def swa_attention(qkv, sinks, *, batch, seq):
    nb = seq // BLOCK
    dq = SWA_HEADS * SWA_HEAD_DIM
    dkv = 2 * SWA_KV_HEADS * LANES
    kv_col = dq // dkv
    return pl.pallas_call(
        _swa_kernel,
        out_shape=jax.ShapeDtypeStruct((batch * seq, dq), BF16),
        grid=(batch, nb),
        in_specs=[
            pl.BlockSpec(memory_space=pltpu.SMEM),
            pl.BlockSpec((BLOCK, dq), lambda b, n: (b * nb + n, 0)),
            pl.BlockSpec((BLOCK, dkv), lambda b, n: (b * nb + jnp.maximum(n - 1, 0), kv_col)),
            pl.BlockSpec((BLOCK, dkv), lambda b, n: (b * nb + n, kv_col)),
        ],
        out_specs=pl.BlockSpec((BLOCK, dq), lambda b, n: (b * nb + n, 0)),
        compiler_params=_params("parallel", "arbitrary"),
        name="swa_attention",
    )(sinks, qkv, qkv, qkv)


SWA_QKV_COLS = SWA_HEADS * SWA_HEAD_DIM + 2 * SWA_KV_HEADS * LANES


def _swa_prepare(w_ref, wb_ref):
    dq = SWA_HEADS * SWA_HEAD_DIM
    wb_ref[:, :dq] = w_ref[:, :dq].astype(BF16)
    low = lax.broadcasted_iota(jnp.int32, (w_ref.shape[0], LANES), 1) < SWA_HEAD_DIM
    for t in range(2):
        x = w_ref[:, dq + t * LANES:dq + (t + 1) * LANES]
        swapped = pltpu.roll(x, SWA_HEAD_DIM, axis=1)
        for g, dup in enumerate((jnp.where(low, x, swapped), jnp.where(low, swapped, x))):
            c = dq + (SWA_KV_HEADS * t + g) * LANES
            wb_ref[:, c:c + LANES] = dup.astype(BF16)


def _sb_kernel(q_ref, k_ref, v_ref, o_ref, kx_ref, vx_ref, run_ref, acc_ref):
    n = pl.program_id(2)
    nkb = k_ref.shape[0] // BLOCK
    sub = SB_TILE // BLOCK
    low = lax.broadcasted_iota(jnp.int32, (nkb, BLOCK, LANES), 2) < SB_HEAD_DIM

    @pl.when(n == 0)
    def _():
        for src, dst in ((k_ref, kx_ref), (v_ref, vx_ref)):
            for pp in range(SB_PAIRS):
                x = src[:, pp * LANES:(pp + 1) * LANES].reshape(nkb, BLOCK, LANES)
                zero = jnp.zeros_like(x)
                dst[pp, :, :BLOCK, :] = jnp.where(low, x, zero)
                dst[pp, :, BLOCK:, :] = jnp.where(low, zero, x)

    q = q_ref[...] * jnp.asarray(SB_HEAD_DIM ** -0.5, BF16)
    tj = lax.broadcasted_iota(jnp.int32, (2 * BLOCK, 2 * BLOCK), 0) % BLOCK
    ts = lax.broadcasted_iota(jnp.int32, (2 * BLOCK, 2 * BLOCK), 1)
    tri = jnp.where((tj > ts) | (ts >= BLOCK), 1.0, 0.0).astype(BF16)
    ti = lax.broadcasted_iota(jnp.int32, (BLOCK, 2 * BLOCK), 0)
    si = lax.broadcasted_iota(jnp.int32, (BLOCK, 2 * BLOCK), 1) % BLOCK
    causal = si < ti
    run_ref[...] = jnp.zeros_like(run_ref)
    acc_ref[...] = jnp.zeros_like(acc_ref)

    def mask_first_block(x):
        head = jnp.where(causal, x[:BLOCK], 0.0)
        return head if x.shape[0] == BLOCK else jnp.concatenate([head, x[BLOCK:]], axis=0)

    def slabs(specs, diag):
        chains = [(pp, kb, r0) for kb, r0 in specs for pp in range(SB_PAIRS)]
        cols = lambda pp: slice(pp * LANES, (pp + 1) * LANES)
        state = [None] * len(chains)

        def scores(i):
            pp, kb, r0 = chains[i]
            state[i] = lax.dot_general(q[r0:, cols(pp)], kx_ref[pp, kb], (((1,), (1,)), ((), ())),
                                       preferred_element_type=F32)

        def prefix(i):
            z = state[i]
            t = jnp.maximum(z, 0.0) + jnp.log(1.0 + jnp.exp2(jnp.abs(z) * -LOG2E))
            log_beta = z - t
            if diag:
                t = mask_first_block(t)
            hi = t.astype(BF16)
            lo = (t - hi.astype(F32)).astype(BF16)
            c = [jnp.dot(jnp.concatenate([hi[:, s:s + BLOCK], lo[:, s:s + BLOCK]], axis=1), tri,
                         preferred_element_type=F32) for s in (0, BLOCK)]
            state[i] = (log_beta, c)

        def values(i):
            pp, kb, r0 = chains[i]
            log_beta, c = state[i]
            cum = jnp.concatenate([c[0][:, :BLOCK], c[1][:, :BLOCK]], axis=1)
            tot = jnp.concatenate([c[0][:, BLOCK:], c[1][:, BLOCK:]], axis=1)
            run = run_ref[pp, r0:, :]
            a = jnp.exp(log_beta - cum - run)
            if diag:
                a = mask_first_block(a)
            acc_ref[r0:, cols(pp)] += jnp.dot(a.astype(BF16), vx_ref[pp, kb], preferred_element_type=F32)
            run_ref[pp, r0:, :] = run + tot
            state[i] = None

        _staggered((scores, prefix, values), len(chains))

    slabs([(n * sub + j, j * BLOCK) for j in reversed(range(sub))], True)

    def tile(carry):
        i, _ = carry
        kb0 = (n - 1 - i) * sub
        slabs([(kb0 + j, 0) for j in reversed(range(sub))], False)
        return i + 1, jnp.min(run_ref[...])

    lax.while_loop(lambda c: (c[0] < n) & (c[1] < SB_EXP_ZERO), tile, (jnp.int32(0), jnp.float32(0.0)))
    o_ref[...] = acc_ref[...].astype(o_ref.dtype)


def sb_attention(qkv, *, batch, seq):
    nt = seq // SB_TILE
    nkb = seq // BLOCK
    width = SB_PAIRS * LANES
    groups = SB_HEADS * SB_HEAD_DIM // width
    return pl.pallas_call(
        _sb_kernel,
        out_shape=jax.ShapeDtypeStruct((batch * seq, SB_HEADS * SB_HEAD_DIM), BF16),
        grid=(batch, groups, nt),
        in_specs=[
            pl.BlockSpec((SB_TILE, width), lambda b, p, n: (b * nt + n, p)),
            pl.BlockSpec((seq, width), lambda b, p, n: (b, groups + p)),
            pl.BlockSpec((seq, width), lambda b, p, n: (b, 2 * groups + p)),
        ],
        out_specs=pl.BlockSpec((SB_TILE, width), lambda b, p, n: (b * nt + n, p)),
        scratch_shapes=[
            pltpu.VMEM((SB_PAIRS, nkb, 2 * BLOCK, LANES), BF16),
            pltpu.VMEM((SB_PAIRS, nkb, 2 * BLOCK, LANES), BF16),
            pltpu.VMEM((SB_PAIRS, SB_TILE, 2 * LANES), F32),
            pltpu.VMEM((SB_TILE, width), F32),
        ],
        compiler_params=_params("parallel", "parallel", "arbitrary"),
        name="sb_attention",
    )(qkv, qkv, qkv)


def _ret_kernel(cd_ref, qkv_ref, g_ref, dm_ref, qd_ref, kd_ref, y_ref, state_ref):
    hh, dk, dv = RET_HEADS, RET_QK_DIM, RET_V_DIM

    @pl.when(pl.program_id(1) == 0)
    def _():
        state_ref[...] = jnp.zeros_like(state_ref)

    chains = [(slice(c * BLOCK, (c + 1) * BLOCK), h) for c in range(RET_STEP_CHUNKS) for h in range(hh)]
    carry = [None] * len(chains)
    value = lambda rows, h: qkv_ref[rows, 2 * hh * dk + h * dv:2 * hh * dk + (h + 1) * dv]

    def scores(i):
        rows, h = chains[i]
        q = qkv_ref[rows, h * dk:(h + 1) * dk]
        kf = qkv_ref[rows, (hh + h) * dk:(hh + h + 1) * dk].astype(F32) * (dk ** -0.5)
        s = lax.dot_general(q, kf.astype(BF16), (((1,), (1,)), ((), ())), preferred_element_type=F32) * dm_ref[h]
        qd = (q.astype(F32) * qd_ref[h]).astype(BF16)
        kd = (kf * kd_ref[h]).astype(BF16)
        carry[i] = (s.astype(BF16), qd, kd)

    def outputs(i):
        rows, h = chains[i]
        s, qd, kd = carry[i]
        v = value(rows, h)
        state = state_ref[h]
        carry[i] = jnp.dot(s, v, preferred_element_type=F32) + jnp.dot(qd, state.astype(BF16),
                                                                      preferred_element_type=F32)
        state_ref[h] = state * cd_ref[h] + lax.dot_general(kd, v, (((0,), (0,)), ((), ())),
                                                           preferred_element_type=F32)

    def gated(i):
        rows, h = chains[i]
        o = carry[i]
        o = o * lax.rsqrt(jnp.mean(o * o, axis=-1, keepdims=True) + RMS_EPS)
        gate = g_ref[rows, h * dv:(h + 1) * dv]
        y_ref[rows, h * dv:(h + 1) * dv] = (gate * (1.0 / (1.0 + jnp.exp(-gate))) * o).astype(y_ref.dtype)
        carry[i] = None

    _staggered((scores, outputs, gated), len(chains))


def _ret_decay_tables():
    c, h = BLOCK, RET_HEADS
    log_gamma = jnp.log1p(-jnp.exp2(-5.0 - jnp.arange(h, dtype=F32)))
    idx = jnp.arange(c)
    diff = idx[:, None] - idx[None, :]
    decay_mat = jnp.where(diff >= 0, jnp.exp(log_gamma[:, None, None] * jnp.maximum(diff, 0).astype(F32)), 0.0)
    q_decay = jnp.exp(log_gamma[:, None] * (idx + 1).astype(F32))[:, :, None]
    k_decay = jnp.exp(log_gamma[:, None] * (c - 1 - idx).astype(F32))[:, :, None]
    chunk_decay = jnp.exp(log_gamma * c)
    return decay_mat, q_decay, k_decay, chunk_decay


def retention(qkv, gate, *, batch, seq):
    rows = RET_STEP_CHUNKS * BLOCK
    nc = seq // rows
    hh, dk, dv = RET_HEADS, RET_QK_DIM, RET_V_DIM
    dm, qd, kd, cd = _ret_decay_tables()
    return pl.pallas_call(
        _ret_kernel,
        out_shape=jax.ShapeDtypeStruct((batch * seq, hh * dv), BF16),
        grid=(batch, nc),
        in_specs=[
            pl.BlockSpec(memory_space=pltpu.SMEM),
            pl.BlockSpec((rows, qkv.shape[1]), lambda b, n: (b * nc + n, 0)),
            pl.BlockSpec((rows, hh * dv), lambda b, n: (b * nc + n, 0)),
            _resident(dm.shape),
            _resident(qd.shape),
            _resident(kd.shape),
        ],
        out_specs=pl.BlockSpec((rows, hh * dv), lambda b, n: (b * nc + n, 0)),
        scratch_shapes=[pltpu.VMEM((hh, dk, dv), F32)],
        compiler_params=_params("parallel", "arbitrary"),
        name="retention",
    )(cd, qkv, gate, dm, qd, kd)


def kernel(x, attn_norm, mlp_norm, final_norm, swa_w_qkv, swa_sinks, swa_w_o, sb_w_qkv, sb_w_o, ret_w_in, ret_w_o,
           mlp_w_up, mlp_w_down):
    batch, seq, d = x.shape
    h = x.reshape(batch * seq, d)
    ret_qkv_dim = 2 * RET_HEADS * RET_QK_DIM + RET_HEADS * RET_V_DIM
    for i in range(DEPTH):
        kind, j = i % N_MIXERS, i // N_MIXERS
        if kind == 0:
            qkv, = norm_proj(h, attn_norm[i], swa_w_qkv[j], [(0, SWA_QKV_COLS, BF16)],
                             prepare=_swa_prepare, prepared_cols=SWA_QKV_COLS)
            mix = swa_attention(qkv, swa_sinks[j], batch=batch, seq=seq)
            w_o = swa_w_o[j]
        elif kind == 1:
            sb_cols = sb_w_qkv.shape[-1]
            qkv, = norm_proj(h, attn_norm[i], sb_w_qkv[j], [(0, sb_cols, BF16)],
                             prepare=_cast_weight, prepared_cols=sb_cols)
            mix = sb_attention(qkv, batch=batch, seq=seq)
            w_o = sb_w_o[j]
        else:
            qkv, gate = norm_proj(h, attn_norm[i], ret_w_in[j].astype(BF16),
                                  [(0, ret_qkv_dim, BF16), (ret_qkv_dim, ret_w_in.shape[-1], F32)])
            mix = retention(qkv, gate, batch=batch, seq=seq)
            w_o = ret_w_o[j]
        h = layer_tail(mix, w_o.astype(BF16), h, mlp_norm[i], mlp_w_up[i].astype(BF16), mlp_w_down[i].astype(BF16),
                       final_gain=final_norm if i == DEPTH - 1 else None)
    return h.reshape(batch, seq, d)
```

```python
import functools

import jax
import jax.numpy as jnp
from jax import lax
from jax.experimental import pallas as pl
from jax.experimental.pallas import tpu as pltpu

D_MODEL = 1024
DEPTH = 4
N_MIXERS = 3
RMS_EPS = 1e-6
BLOCK = 128
NEG = -1e30

SWA_HEAD_DIM = 64
SWA_HEADS = 16
SWA_KV_HEADS = 2
SWA_GROUP = 8
SB_HEAD_DIM = 64
SB_HEADS = 16
SB_TILE = 256
SB_PAIRS = 4
SB_EXP_ZERO = 104.0
RET_QK_DIM = 256
RET_HEADS = 4
RET_V_DIM = 512
RET_STEP_CHUNKS = 4
D_FF = 4 * D_MODEL

LANES = 128
VMEM_LIMIT = 48 * 1024 * 1024
TOKEN_TILE = 512
COL_CHUNK = 1024
FF_TILE = 2048
LOG2E = 1.4426950408889634

F32 = jnp.float32
BF16 = jnp.bfloat16


def _params(*sem):
    return pltpu.CompilerParams(dimension_semantics=sem, vmem_limit_bytes=VMEM_LIMIT)


def _rms(x, g):
    return x * lax.rsqrt(jnp.mean(x * x, axis=-1, keepdims=True) + RMS_EPS) * g


def _staggered(stages, n):
    for tick in range(n + len(stages) - 1):
        for depth, stage in enumerate(stages):
            if 0 <= tick - depth < n:
                stage(tick - depth)


def _resident(shape):
    return pl.BlockSpec(shape, lambda *_: (0,) * len(shape), pipeline_mode=pl.Buffered(1))


def _resident_layer(stack, layer):
    return pl.BlockSpec((None,) + stack.shape[1:], lambda *_: (layer, 0, 0), pipeline_mode=pl.Buffered(1))


def _norm_proj_kernel(outs, prepare, x_ref, g_ref, w_ref, *refs):
    o_refs = refs[:len(outs)]
    if prepare is not None:
        wb_ref = refs[len(outs)]

        @pl.when(pl.program_id(0) == 0)
        def _():
            prepare(w_ref, wb_ref)
    else:
        wb_ref = w_ref
    xn = _rms(x_ref[...], g_ref[...]).astype(BF16)
    for (c0, c1, _), o_ref in zip(outs, o_refs):
        for c in range(c0, c1, COL_CHUNK):
            ce = min(c + COL_CHUNK, c1)
            o_ref[:, c - c0:ce - c0] = jnp.dot(xn, wb_ref[:, c:ce], preferred_element_type=F32).astype(o_ref.dtype)


def norm_proj(x, g, w_stack, layer, outs, prepare=None, prepared_cols=None):
    n, d = x.shape
    tm = TOKEN_TILE
    scratch = [] if prepare is None else [pltpu.VMEM((d, prepared_cols), BF16)]
    return pl.pallas_call(
        functools.partial(_norm_proj_kernel, tuple(outs), prepare),
        out_shape=[jax.ShapeDtypeStruct((n, c1 - c0), dt) for c0, c1, dt in outs],
        grid=(n // tm,),
        in_specs=[pl.BlockSpec((tm, d), lambda i: (i, 0)), _resident((1, d)), _resident_layer(w_stack, layer)],
        out_specs=[pl.BlockSpec((tm, c1 - c0), lambda i: (i, 0)) for c0, c1, _ in outs],
        scratch_shapes=scratch,
        compiler_params=_params("parallel" if prepare is None else "arbitrary"),
        name="norm_proj",
    )(x, g.reshape(1, d), w_stack)


def _cast_weight(w_ref, wb_ref):
    wb_ref[...] = w_ref[...].astype(BF16)


def _tail_kernel(final, mix_ref, wo_ref, h_ref, g_ref, wu_ref, wd_ref, *refs):
    if final:
        fg_ref, o_ref, xn_ref = refs
    else:
        o_ref, xn_ref = refs
    j = pl.program_id(1)

    @pl.when(j == 0)
    def _():
        h1 = h_ref[...] + jnp.dot(mix_ref[...], wo_ref[...], preferred_element_type=F32)
        xn_ref[...] = _rms(h1, g_ref[...]).astype(BF16)
        o_ref[...] = h1

    a = jnp.maximum(jnp.dot(xn_ref[...], wu_ref[...], preferred_element_type=F32), 0.0)
    o_ref[...] += jnp.dot((a * a).astype(BF16), wd_ref[...], preferred_element_type=F32)

    if final:
        @pl.when(j == pl.num_programs(1) - 1)
        def _():
            o_ref[...] = _rms(o_ref[...], fg_ref[...])


def layer_tail(mix, w_o, o_layer, h, g, w_up, w_down, layer, final_gain=None):
    n, d = h.shape
    k = mix.shape[1]
    dff = w_up.shape[2]
    tm, tf = TOKEN_TILE, FF_TILE
    final = final_gain is not None
    in_specs = [
        pl.BlockSpec((tm, k), lambda i, j: (i, 0)),
        _resident_layer(w_o, o_layer),
        pl.BlockSpec((tm, d), lambda i, j: (i, 0)),
        _resident((1, d)),
        pl.BlockSpec((None, d, tf), lambda i, j: (layer, 0, j)),
        pl.BlockSpec((None, tf, d), lambda i, j: (layer, j, 0)),
    ]
    args = [mix, w_o, h, g.reshape(1, d), w_up, w_down]
    if final:
        in_specs.append(_resident((1, d)))
        args.append(final_gain.reshape(1, d))
    return pl.pallas_call(
        functools.partial(_tail_kernel, final),
        out_shape=jax.ShapeDtypeStruct((n, d), F32),
        grid=(n // tm, dff // tf),
        in_specs=in_specs,
        out_specs=pl.BlockSpec((tm, d), lambda i, j: (i, 0)),
        scratch_shapes=[pltpu.VMEM((tm, d), BF16)],
        compiler_params=_params("parallel", "arbitrary"),
        name="layer_tail",
    )(*args)


def _swa_kernel(sink_ref, q_ref, kvp_ref, kvc_ref, o_ref):
    n = pl.program_id(1)
    q = q_ref[...] * jnp.asarray(SWA_HEAD_DIM ** -0.5, BF16)
    kv = jnp.concatenate([kvp_ref[...], kvc_ref[...]], axis=0)
    qi = lax.broadcasted_iota(jnp.int32, (BLOCK, BLOCK), 0)
    ci = lax.broadcasted_iota(jnp.int32, (BLOCK, BLOCK), 1)
    upper = ci > qi
    valid = jnp.logical_not(upper) | (n > 0)
    distf = jnp.where(upper, BLOCK + qi - ci, qi - ci).astype(F32)
    low = lax.broadcasted_iota(jnp.int32, (BLOCK, LANES), 1) < SWA_HEAD_DIM
    state = [None] * SWA_HEADS
    outs = [None] * SWA_HEADS

    def scores(h):
        g, c0 = h // SWA_GROUP, (h - h % 2) * SWA_HEAD_DIM
        q2 = q[:, c0:c0 + LANES]
        qm = jnp.where(low if h % 2 == 0 else jnp.logical_not(low), q2, jnp.zeros_like(q2))
        state[h] = lax.dot_general(qm, kv[:, g * LANES:(g + 1) * LANES], (((1,), (1,)), ((), ())),
                                   preferred_element_type=F32)

    def softmax(h):
        s = state[h]
        slope = 2.0 ** (-8.0 * (h + 1) / SWA_HEADS)
        s = jnp.where(upper, s[:, :BLOCK], s[:, BLOCK:])
        logits = jnp.where(valid, s - slope * distf, NEG)
        sink = sink_ref[h]
        m = jnp.maximum(jnp.max(logits, axis=-1, keepdims=True), sink)
        pexp = jnp.exp(logits - m)
        inv = 1.0 / (jnp.sum(pexp, axis=-1, keepdims=True) + jnp.exp(sink - m))
        pb = pexp.astype(BF16)
        zero = jnp.zeros_like(pb)
        state[h] = (jnp.concatenate([jnp.where(upper, pb, zero), jnp.where(upper, zero, pb)], axis=1), inv)

    def values(h):
        g, c0 = h // SWA_GROUP, (h - h % 2) * SWA_HEAD_DIM
        p2, inv = state[h]
        vv = kv[:, (SWA_KV_HEADS + g) * LANES:(SWA_KV_HEADS + g + 1) * LANES]
        outs[h] = jnp.dot(p2, vv, preferred_element_type=F32) * inv
        state[h] = None
        if h % 2 == 1:
            o_ref[:, c0:c0 + LANES] = jnp.where(low, outs[h - 1], outs[h]).astype(o_ref.dtype)
            outs[h - 1] = outs[h] = None

    _staggered((scores, softmax, values), SWA_HEADS)


def swa_attention(qkv, sinks, *, batch, seq):
    nb = seq // BLOCK
    dq = SWA_HEADS * SWA_HEAD_DIM
    dkv = 2 * SWA_KV_HEADS * LANES
    kv_col = dq // dkv
    return pl.pallas_call(
        _swa_kernel,
        out_shape=jax.ShapeDtypeStruct((batch * seq, dq), BF16),
        grid=(batch, nb),
        in_specs=[
            pl.BlockSpec(memory_space=pltpu.SMEM),
            pl.BlockSpec((BLOCK, dq), lambda b, n: (b * nb + n, 0)),
            pl.BlockSpec((BLOCK, dkv), lambda b, n: (b * nb + jnp.maximum(n - 1, 0), kv_col)),
            pl.BlockSpec((BLOCK, dkv), lambda b, n: (b * nb + n, kv_col)),
        ],
        out_specs=pl.BlockSpec((BLOCK, dq), lambda b, n: (b * nb + n, 0)),
        compiler_params=_params("parallel", "arbitrary"),
        name="swa_attention",
    )(sinks, qkv, qkv, qkv)


SWA_QKV_COLS = SWA_HEADS * SWA_HEAD_DIM + 2 * SWA_KV_HEADS * LANES


def _swa_prepare(w_ref, wb_ref):
    dq = SWA_HEADS * SWA_HEAD_DIM
    wb_ref[:, :dq] = w_ref[:, :dq].astype(BF16)
    low = lax.broadcasted_iota(jnp.int32, (w_ref.shape[0], LANES), 1) < SWA_HEAD_DIM
    for t in range(2):
        x = w_ref[:, dq + t * LANES:dq + (t + 1) * LANES]
        swapped = pltpu.roll(x, SWA_HEAD_DIM, axis=1)
        for g, dup in enumerate((jnp.where(low, x, swapped), jnp.where(low, swapped, x))):
            c = dq + (SWA_KV_HEADS * t + g) * LANES
            wb_ref[:, c:c + LANES] = dup.astype(BF16)


def _sb_kernel(q_ref, k_ref, v_ref, o_ref, kx_ref, vx_ref, run_ref, acc_ref):
    n = pl.program_id(2)
    nkb = k_ref.shape[0] // BLOCK
    sub = SB_TILE // BLOCK
    low = lax.broadcasted_iota(jnp.int32, (nkb, BLOCK, LANES), 2) < SB_HEAD_DIM

    @pl.when(n == 0)
    def _():
        for src, dst in ((k_ref, kx_ref), (v_ref, vx_ref)):
            for pp in range(SB_PAIRS):
                x = src[:, pp * LANES:(pp + 1) * LANES].reshape(nkb, BLOCK, LANES)
                zero = jnp.zeros_like(x)
                dst[pp, :, :BLOCK, :] = jnp.where(low, x, zero)
                dst[pp, :, BLOCK:, :] = jnp.where(low, zero, x)

    q = q_ref[...] * jnp.asarray(SB_HEAD_DIM ** -0.5, BF16)
    tj = lax.broadcasted_iota(jnp.int32, (2 * BLOCK, 2 * BLOCK), 0) % BLOCK
    ts = lax.broadcasted_iota(jnp.int32, (2 * BLOCK, 2 * BLOCK), 1)
    tri = jnp.where((tj > ts) | (ts >= BLOCK), 1.0, 0.0).astype(BF16)
    ti = lax.broadcasted_iota(jnp.int32, (BLOCK, 2 * BLOCK), 0)
    si = lax.broadcasted_iota(jnp.int32, (BLOCK, 2 * BLOCK), 1) % BLOCK
    causal = si < ti
    run_ref[...] = jnp.zeros_like(run_ref)
    acc_ref[...] = jnp.zeros_like(acc_ref)

    def mask_first_block(x):
        head = jnp.where(causal, x[:BLOCK], 0.0)
        return head if x.shape[0] == BLOCK else jnp.concatenate([head, x[BLOCK:]], axis=0)

    def slabs(specs, diag):
        chains = [(pp, kb, r0) for kb, r0 in specs for pp in range(SB_PAIRS)]
        cols = lambda pp: slice(pp * LANES, (pp + 1) * LANES)
        state = [None] * len(chains)

        def scores(i):
            pp, kb, r0 = chains[i]
            state[i] = lax.dot_general(q[r0:, cols(pp)], kx_ref[pp, kb], (((1,), (1,)), ((), ())),
                                       preferred_element_type=F32)

        def prefix(i):
            z = state[i]
            t = jnp.maximum(z, 0.0) + jnp.log(1.0 + jnp.exp2(jnp.abs(z) * -LOG2E))
            log_beta = z - t
            if diag:
                t = mask_first_block(t)
            hi = t.astype(BF16)
            lo = (t - hi.astype(F32)).astype(BF16)
            c = [jnp.dot(jnp.concatenate([hi[:, s:s + BLOCK], lo[:, s:s + BLOCK]], axis=1), tri,
                         preferred_element_type=F32) for s in (0, BLOCK)]
            state[i] = (log_beta, c)

        def values(i):
            pp, kb, r0 = chains[i]
            log_beta, c = state[i]
            cum = jnp.concatenate([c[0][:, :BLOCK], c[1][:, :BLOCK]], axis=1)
            tot = jnp.concatenate([c[0][:, BLOCK:], c[1][:, BLOCK:]], axis=1)
            run = run_ref[pp, r0:, :]
            a = jnp.exp(log_beta - cum - run)
            if diag:
                a = mask_first_block(a)
            acc_ref[r0:, cols(pp)] += jnp.dot(a.astype(BF16), vx_ref[pp, kb], preferred_element_type=F32)
            run_ref[pp, r0:, :] = run + tot
            state[i] = None

        _staggered((scores, prefix, values), len(chains))

    slabs([(n * sub + j, j * BLOCK) for j in reversed(range(sub))], True)

    def tile(carry):
        i, _ = carry
        kb0 = (n - 1 - i) * sub
        slabs([(kb0 + j, 0) for j in reversed(range(sub))], False)
        return i + 1, jnp.min(run_ref[...])

    lax.while_loop(lambda c: (c[0] < n) & (c[1] < SB_EXP_ZERO), tile, (jnp.int32(0), jnp.float32(0.0)))
    o_ref[...] = acc_ref[...].astype(o_ref.dtype)


def sb_attention(qkv, *, batch, seq):
    nt = seq // SB_TILE
    nkb = seq // BLOCK
    width = SB_PAIRS * LANES
    groups = SB_HEADS * SB_HEAD_DIM // width
    return pl.pallas_call(
        _sb_kernel,
        out_shape=jax.ShapeDtypeStruct((batch * seq, SB_HEADS * SB_HEAD_DIM), BF16),
        grid=(batch, groups, nt),
        in_specs=[
            pl.BlockSpec((SB_TILE, width), lambda b, p, n: (b * nt + n, p)),
            pl.BlockSpec((seq, width), lambda b, p, n: (b, groups + p)),
            pl.BlockSpec((seq, width), lambda b, p, n: (b, 2 * groups + p)),
        ],
        out_specs=pl.BlockSpec((SB_TILE, width), lambda b, p, n: (b * nt + n, p)),
        scratch_shapes=[
            pltpu.VMEM((SB_PAIRS, nkb, 2 * BLOCK, LANES), BF16),
            pltpu.VMEM((SB_PAIRS, nkb, 2 * BLOCK, LANES), BF16),
            pltpu.VMEM((SB_PAIRS, SB_TILE, 2 * LANES), F32),
            pltpu.VMEM((SB_TILE, width), F32),
        ],
        compiler_params=_params("parallel", "parallel", "arbitrary"),
        name="sb_attention",
    )(qkv, qkv, qkv)


def _ret_kernel(cd_ref, qkv_ref, g_ref, dm_ref, qd_ref, kd_ref, y_ref, state_ref):
    hh, dk, dv = RET_HEADS, RET_QK_DIM, RET_V_DIM

    @pl.when(pl.program_id(1) == 0)
    def _():
        state_ref[...] = jnp.zeros_like(state_ref)

    chains = [(slice(c * BLOCK, (c + 1) * BLOCK), h) for c in range(RET_STEP_CHUNKS) for h in range(hh)]
    carry = [None] * len(chains)
    value = lambda rows, h: qkv_ref[rows, 2 * hh * dk + h * dv:2 * hh * dk + (h + 1) * dv]

    def scores(i):
        rows, h = chains[i]
        q = qkv_ref[rows, h * dk:(h + 1) * dk]
        kf = qkv_ref[rows, (hh + h) * dk:(hh + h + 1) * dk].astype(F32) * (dk ** -0.5)
        s = lax.dot_general(q, kf.astype(BF16), (((1,), (1,)), ((), ())), preferred_element_type=F32) * dm_ref[h]
        qd = (q.astype(F32) * qd_ref[h]).astype(BF16)
        kd = (kf * kd_ref[h]).astype(BF16)
        carry[i] = (s.astype(BF16), qd, kd)

    def outputs(i):
        rows, h = chains[i]
        s, qd, kd = carry[i]
        v = value(rows, h)
        state = state_ref[h]
        carry[i] = jnp.dot(s, v, preferred_element_type=F32) + jnp.dot(qd, state.astype(BF16),
                                                                      preferred_element_type=F32)
        state_ref[h] = state * cd_ref[h] + lax.dot_general(kd, v, (((0,), (0,)), ((), ())),
                                                           preferred_element_type=F32)

    def gated(i):
        rows, h = chains[i]
        o = carry[i]
        o = o * lax.rsqrt(jnp.mean(o * o, axis=-1, keepdims=True) + RMS_EPS)
        gate = g_ref[rows, h * dv:(h + 1) * dv]
        y_ref[rows, h * dv:(h + 1) * dv] = (gate * (1.0 / (1.0 + jnp.exp(-gate))) * o).astype(y_ref.dtype)
        carry[i] = None

    _staggered((scores, outputs, gated), len(chains))


def _ret_decay_tables():
    c, h = BLOCK, RET_HEADS
    log_gamma = jnp.log1p(-jnp.exp2(-5.0 - jnp.arange(h, dtype=F32)))
    idx = jnp.arange(c)
    diff = idx[:, None] - idx[None, :]
    decay_mat = jnp.where(diff >= 0, jnp.exp(log_gamma[:, None, None] * jnp.maximum(diff, 0).astype(F32)), 0.0)
    q_decay = jnp.exp(log_gamma[:, None] * (idx + 1).astype(F32))[:, :, None]
    k_decay = jnp.exp(log_gamma[:, None] * (c - 1 - idx).astype(F32))[:, :, None]
    chunk_decay = jnp.exp(log_gamma * c)
    return decay_mat, q_decay, k_decay, chunk_decay


def retention(qkv, gate, *, batch, seq):
    rows = RET_STEP_CHUNKS * BLOCK
    nc = seq // rows
    hh, dk, dv = RET_HEADS, RET_QK_DIM, RET_V_DIM
    dm, qd, kd, cd = _ret_decay_tables()
    return pl.pallas_call(
        _ret_kernel,
        out_shape=jax.ShapeDtypeStruct((batch * seq, hh * dv), BF16),
        grid=(batch, nc),
        in_specs=[
            pl.BlockSpec(memory_space=pltpu.SMEM),
            pl.BlockSpec((rows, qkv.shape[1]), lambda b, n: (b * nc + n, 0)),
            pl.BlockSpec((rows, hh * dv), lambda b, n: (b * nc + n, 0)),
            _resident(dm.shape),
            _resident(qd.shape),
            _resident(kd.shape),
        ],
        out_specs=pl.BlockSpec((rows, hh * dv), lambda b, n: (b * nc + n, 0)),
        scratch_shapes=[pltpu.VMEM((hh, dk, dv), F32)],
        compiler_params=_params("parallel", "arbitrary"),
        name="retention",
    )(cd, qkv, gate, dm, qd, kd)


def kernel(x, attn_norm, mlp_norm, final_norm, swa_w_qkv, swa_sinks, swa_w_o, sb_w_qkv, sb_w_o, ret_w_in, ret_w_o,
           mlp_w_up, mlp_w_down):
    batch, seq, d = x.shape
    h = x.reshape(batch * seq, d)
    ret_qkv_dim = 2 * RET_HEADS * RET_QK_DIM + RET_HEADS * RET_V_DIM
    ret_w_in_b, mlp_w_up_b, mlp_w_down_b = ret_w_in.astype(BF16), mlp_w_up.astype(BF16), mlp_w_down.astype(BF16)
    w_o_b = [w.astype(BF16) for w in (swa_w_o, sb_w_o, ret_w_o)]
    for i in range(DEPTH):
        kind, j = i % N_MIXERS, i // N_MIXERS
        if kind == 0:
            qkv, = norm_proj(h, attn_norm[i], swa_w_qkv, j, [(0, SWA_QKV_COLS, BF16)],
                             prepare=_swa_prepare, prepared_cols=SWA_QKV_COLS)
            mix = swa_attention(qkv, swa_sinks[j], batch=batch, seq=seq)
        elif kind == 1:
            sb_cols = sb_w_qkv.shape[-1]
            qkv, = norm_proj(h, attn_norm[i], sb_w_qkv, j, [(0, sb_cols, BF16)],
                             prepare=_cast_weight, prepared_cols=sb_cols)
            mix = sb_attention(qkv, batch=batch, seq=seq)
        else:
            qkv, gate = norm_proj(h, attn_norm[i], ret_w_in_b, j,
                                  [(0, ret_qkv_dim, BF16), (ret_qkv_dim, ret_w_in.shape[-1], F32)])
            mix = retention(qkv, gate, batch=batch, seq=seq)
        h = layer_tail(mix, w_o_b[kind], j, h, mlp_norm[i], mlp_w_up_b, mlp_w_down_b, i,
                       final_gain=final_norm if i == DEPTH - 1 else None)
    return h.reshape(batch, seq, d)
```

```python
import functools

import jax
import jax.numpy as jnp
from jax import lax
from jax.experimental import pallas as pl
from jax.experimental.pallas import tpu as pltpu

D_MODEL = 1024
DEPTH = 4
N_MIXERS = 3
RMS_EPS = 1e-6
BLOCK = 128
NEG = -1e30

SWA_HEAD_DIM = 64
SWA_HEADS = 16
SWA_KV_HEADS = 2
SWA_GROUP = 8
SB_HEAD_DIM = 64
SB_HEADS = 16
SB_TILE = 256
SB_PAIRS = 4
SB_EXP_ZERO = 104.0
RET_QK_DIM = 256
RET_HEADS = 4
RET_V_DIM = 512
RET_STEP_CHUNKS = 4
D_FF = 4 * D_MODEL

LANES = 128
VMEM_LIMIT = 48 * 1024 * 1024
TOKEN_TILE = 512
COL_CHUNK = 1024
LOG2E = 1.4426950408889634

F32 = jnp.float32
BF16 = jnp.bfloat16


def _params(*sem):
    return pltpu.CompilerParams(dimension_semantics=sem, vmem_limit_bytes=VMEM_LIMIT)


def _rms(x, g):
    return x * lax.rsqrt(jnp.mean(x * x, axis=-1, keepdims=True) + RMS_EPS) * g


def _staggered(stages, n):
    for tick in range(n + len(stages) - 1):
        for depth, stage in enumerate(stages):
            if 0 <= tick - depth < n:
                stage(tick - depth)


def _resident(shape):
    return pl.BlockSpec(shape, lambda *_: (0,) * len(shape), pipeline_mode=pl.Buffered(1))


def _resident_layer(stack, layer):
    return pl.BlockSpec((None,) + stack.shape[1:], lambda *_: (layer, 0, 0), pipeline_mode=pl.Buffered(1))


def _norm_proj_kernel(outs, prepare, x_ref, g_ref, w_ref, *refs):
    o_refs = refs[:len(outs)]
    if prepare is not None:
        wb_ref = refs[len(outs)]

        @pl.when(pl.program_id(0) == 0)
        def _():
            prepare(w_ref, wb_ref)
    else:
        wb_ref = w_ref
    xn = _rms(x_ref[...], g_ref[...]).astype(BF16)
    for (c0, c1, _), o_ref in zip(outs, o_refs):
        for c in range(c0, c1, COL_CHUNK):
            ce = min(c + COL_CHUNK, c1)
            o_ref[:, c - c0:ce - c0] = jnp.dot(xn, wb_ref[:, c:ce], preferred_element_type=F32).astype(o_ref.dtype)


def norm_proj(x, g, w_stack, layer, outs, prepare=None, prepared_cols=None):
    n, d = x.shape
    tm = TOKEN_TILE
    scratch = [] if prepare is None else [pltpu.VMEM((d, prepared_cols), BF16)]
    return pl.pallas_call(
        functools.partial(_norm_proj_kernel, tuple(outs), prepare),
        out_shape=[jax.ShapeDtypeStruct((n, c1 - c0), dt) for c0, c1, dt in outs],
        grid=(n // tm,),
        in_specs=[pl.BlockSpec((tm, d), lambda i: (i, 0)), _resident((1, d)), _resident_layer(w_stack, layer)],
        out_specs=[pl.BlockSpec((tm, c1 - c0), lambda i: (i, 0)) for c0, c1, _ in outs],
        scratch_shapes=scratch,
        compiler_params=_params("parallel" if prepare is None else "arbitrary"),
        name="norm_proj",
    )(x, g.reshape(1, d), w_stack)


def _cast_weight(w_ref, wb_ref):
    wb_ref[...] = w_ref[...].astype(BF16)


def _tail_kernel(final, mix_ref, wo_ref, h_ref, g_ref, wu_ref, wd_ref, *refs):
    if final:
        fg_ref, o_ref = refs
    else:
        o_ref, = refs
    h1 = h_ref[...] + jnp.dot(mix_ref[...], wo_ref[...], preferred_element_type=F32)
    xn = _rms(h1, g_ref[...]).astype(BF16)
    a = jnp.maximum(jnp.dot(xn, wu_ref[...], preferred_element_type=F32), 0.0)
    out = h1 + jnp.dot((a * a).astype(BF16), wd_ref[...], preferred_element_type=F32)
    o_ref[...] = _rms(out, fg_ref[...]) if final else out


def layer_tail(mix, w_o, o_layer, h, g, w_up, w_down, layer, final_gain=None):
    n, d = h.shape
    k = mix.shape[1]
    tm = TOKEN_TILE
    final = final_gain is not None
    in_specs = [
        pl.BlockSpec((tm, k), lambda i: (i, 0)),
        _resident_layer(w_o, o_layer),
        pl.BlockSpec((tm, d), lambda i: (i, 0)),
        _resident((1, d)),
        _resident_layer(w_up, layer),
        _resident_layer(w_down, layer),
    ]
    args = [mix, w_o, h, g.reshape(1, d), w_up, w_down]
    if final:
        in_specs.append(_resident((1, d)))
        args.append(final_gain.reshape(1, d))
    return pl.pallas_call(
        functools.partial(_tail_kernel, final),
        out_shape=jax.ShapeDtypeStruct((n, d), F32),
        grid=(n // tm,),
        in_specs=in_specs,
        out_specs=pl.BlockSpec((tm, d), lambda i: (i, 0)),
        compiler_params=_params("parallel"),
        name="layer_tail",
    )(*args)


def _swa_kernel(sink_ref, q_ref, kvp_ref, kvc_ref, o_ref):
    n = pl.program_id(1)
    q = q_ref[...] * jnp.asarray(SWA_HEAD_DIM ** -0.5, BF16)
    kv = jnp.concatenate([kvp_ref[...], kvc_ref[...]], axis=0)
    qi = lax.broadcasted_iota(jnp.int32, (BLOCK, BLOCK), 0)
    ci = lax.broadcasted_iota(jnp.int32, (BLOCK, BLOCK), 1)
    upper = ci > qi
    valid = jnp.logical_not(upper) | (n > 0)
    distf = jnp.where(upper, BLOCK + qi - ci, qi - ci).astype(F32)
    low = lax.broadcasted_iota(jnp.int32, (BLOCK, LANES), 1) < SWA_HEAD_DIM
    state = [None] * SWA_HEADS
    outs = [None] * SWA_HEADS

    def scores(h):
        g, c0 = h // SWA_GROUP, (h - h % 2) * SWA_HEAD_DIM
        q2 = q[:, c0:c0 + LANES]
        qm = jnp.where(low if h % 2 == 0 else jnp.logical_not(low), q2, jnp.zeros_like(q2))
        state[h] = lax.dot_general(qm, kv[:, g * LANES:(g + 1) * LANES], (((1,), (1,)), ((), ())),
                                   preferred_element_type=F32)

    def softmax(h):
        s = state[h]
        slope = 2.0 ** (-8.0 * (h + 1) / SWA_HEADS)
        s = jnp.where(upper, s[:, :BLOCK], s[:, BLOCK:])
        logits = jnp.where(valid, s - slope * distf, NEG)
        sink = sink_ref[h]
        m = jnp.maximum(jnp.max(logits, axis=-1, keepdims=True), sink)
        pexp = jnp.exp(logits - m)
        inv = 1.0 / (jnp.sum(pexp, axis=-1, keepdims=True) + jnp.exp(sink - m))
        pb = pexp.astype(BF16)
        zero = jnp.zeros_like(pb)
        state[h] = (jnp.concatenate([jnp.where(upper, pb, zero), jnp.where(upper, zero, pb)], axis=1), inv)

    def values(h):
        g, c0 = h // SWA_GROUP, (h - h % 2) * SWA_HEAD_DIM
        p2, inv = state[h]
        vv = kv[:, (SWA_KV_HEADS + g) * LANES:(SWA_KV_HEADS + g + 1) * LANES]
        outs[h] = jnp.dot(p2, vv, preferred_element_type=F32) * inv
        state[h] = None
        if h % 2 == 1:
            o_ref[:, c0:c0 + LANES] = jnp.where(low, outs[h - 1], outs[h]).astype(o_ref.dtype)
            outs[h - 1] = outs[h] = None

    _staggered((scores, softmax, values), SWA_HEADS)


def swa_attention(qkv, sinks, *, batch, seq):
    nb = seq // BLOCK
    dq = SWA_HEADS * SWA_HEAD_DIM
    dkv = 2 * SWA_KV_HEADS * LANES
    kv_col = dq // dkv
    return pl.pallas_call(
        _swa_kernel,
        out_shape=jax.ShapeDtypeStruct((batch * seq, dq), BF16),
        grid=(batch, nb),
        in_specs=[
            pl.BlockSpec(memory_space=pltpu.SMEM),
            pl.BlockSpec((BLOCK, dq), lambda b, n: (b * nb + n, 0)),
            pl.BlockSpec((BLOCK, dkv), lambda b, n: (b * nb + jnp.maximum(n - 1, 0), kv_col)),
            pl.BlockSpec((BLOCK, dkv), lambda b, n: (b * nb + n, kv_col)),
        ],
        out_specs=pl.BlockSpec((BLOCK, dq), lambda b, n: (b * nb + n, 0)),
        compiler_params=_params("parallel", "arbitrary"),
        name="swa_attention",
    )(sinks, qkv, qkv, qkv)


SWA_QKV_COLS = SWA_HEADS * SWA_HEAD_DIM + 2 * SWA_KV_HEADS * LANES


def _swa_prepare(w_ref, wb_ref):
    dq = SWA_HEADS * SWA_HEAD_DIM
    wb_ref[:, :dq] = w_ref[:, :dq].astype(BF16)
    low = lax.broadcasted_iota(jnp.int32, (w_ref.shape[0], LANES), 1) < SWA_HEAD_DIM
    for t in range(2):
        x = w_ref[:, dq + t * LANES:dq + (t + 1) * LANES]
        swapped = pltpu.roll(x, SWA_HEAD_DIM, axis=1)
        for g, dup in enumerate((jnp.where(low, x, swapped), jnp.where(low, swapped, x))):
            c = dq + (SWA_KV_HEADS * t + g) * LANES
            wb_ref[:, c:c + LANES] = dup.astype(BF16)


def _sb_kernel(q_ref, k_ref, v_ref, o_ref, kx_ref, vx_ref, run_ref, acc_ref):
    n = pl.program_id(2)
    nkb = k_ref.shape[0] // BLOCK
    sub = SB_TILE // BLOCK
    low = lax.broadcasted_iota(jnp.int32, (nkb, BLOCK, LANES), 2) < SB_HEAD_DIM

    @pl.when(n == 0)
    def _():
        for src, dst in ((k_ref, kx_ref), (v_ref, vx_ref)):
            for pp in range(SB_PAIRS):
                x = src[:, pp * LANES:(pp + 1) * LANES].reshape(nkb, BLOCK, LANES)
                zero = jnp.zeros_like(x)
                dst[pp, :, :BLOCK, :] = jnp.where(low, x, zero)
                dst[pp, :, BLOCK:, :] = jnp.where(low, zero, x)

    q = q_ref[...] * jnp.asarray(SB_HEAD_DIM ** -0.5, BF16)
    tj = lax.broadcasted_iota(jnp.int32, (2 * BLOCK, 2 * BLOCK), 0) % BLOCK
    ts = lax.broadcasted_iota(jnp.int32, (2 * BLOCK, 2 * BLOCK), 1)
    tri = jnp.where((tj > ts) | (ts >= BLOCK), 1.0, 0.0).astype(BF16)
    ti = lax.broadcasted_iota(jnp.int32, (BLOCK, 2 * BLOCK), 0)
    si = lax.broadcasted_iota(jnp.int32, (BLOCK, 2 * BLOCK), 1) % BLOCK
    causal = si < ti
    run_ref[...] = jnp.zeros_like(run_ref)
    acc_ref[...] = jnp.zeros_like(acc_ref)

    def mask_first_block(x):
        head = jnp.where(causal, x[:BLOCK], 0.0)
        return head if x.shape[0] == BLOCK else jnp.concatenate([head, x[BLOCK:]], axis=0)

    def slabs(specs, diag):
        chains = [(pp, kb, r0) for kb, r0 in specs for pp in range(SB_PAIRS)]
        cols = lambda pp: slice(pp * LANES, (pp + 1) * LANES)
        state = [None] * len(chains)

        def scores(i):
            pp, kb, r0 = chains[i]
            state[i] = lax.dot_general(q[r0:, cols(pp)], kx_ref[pp, kb], (((1,), (1,)), ((), ())),
                                       preferred_element_type=F32)

        def prefix(i):
            z = state[i]
            t = jnp.maximum(z, 0.0) + jnp.log(1.0 + jnp.exp2(jnp.abs(z) * -LOG2E))
            log_beta = z - t
            if diag:
                t = mask_first_block(t)
            hi = t.astype(BF16)
            lo = (t - hi.astype(F32)).astype(BF16)
            c = [jnp.dot(jnp.concatenate([hi[:, s:s + BLOCK], lo[:, s:s + BLOCK]], axis=1), tri,
                         preferred_element_type=F32) for s in (0, BLOCK)]
            state[i] = (log_beta, c)

        def values(i):
            pp, kb, r0 = chains[i]
            log_beta, c = state[i]
            cum = jnp.concatenate([c[0][:, :BLOCK], c[1][:, :BLOCK]], axis=1)
            tot = jnp.concatenate([c[0][:, BLOCK:], c[1][:, BLOCK:]], axis=1)
            run = run_ref[pp, r0:, :]
            a = jnp.exp(log_beta - cum - run)
            if diag:
                a = mask_first_block(a)
            acc_ref[r0:, cols(pp)] += jnp.dot(a.astype(BF16), vx_ref[pp, kb], preferred_element_type=F32)
            run_ref[pp, r0:, :] = run + tot
            state[i] = None

        _staggered((scores, prefix, values), len(chains))

    slabs([(n * sub + j, j * BLOCK) for j in reversed(range(sub))], True)

    def tile(carry):
        i, _ = carry
        kb0 = (n - 1 - i) * sub
        slabs([(kb0 + j, 0) for j in reversed(range(sub))], False)
        return i + 1, jnp.min(run_ref[...])

    lax.while_loop(lambda c: (c[0] < n) & (c[1] < SB_EXP_ZERO), tile, (jnp.int32(0), jnp.float32(0.0)))
    o_ref[...] = acc_ref[...].astype(o_ref.dtype)


def sb_attention(qkv, *, batch, seq):
    nt = seq // SB_TILE
    nkb = seq // BLOCK
    width = SB_PAIRS * LANES
    groups = SB_HEADS * SB_HEAD_DIM // width
    return pl.pallas_call(
        _sb_kernel,
        out_shape=jax.ShapeDtypeStruct((batch * seq, SB_HEADS * SB_HEAD_DIM), BF16),
        grid=(batch, groups, nt),
        in_specs=[
            pl.BlockSpec((SB_TILE, width), lambda b, p, n: (b * nt + n, p)),
            pl.BlockSpec((seq, width), lambda b, p, n: (b, groups + p)),
            pl.BlockSpec((seq, width), lambda b, p, n: (b, 2 * groups + p)),
        ],
        out_specs=pl.BlockSpec((SB_TILE, width), lambda b, p, n: (b * nt + n, p)),
        scratch_shapes=[
            pltpu.VMEM((SB_PAIRS, nkb, 2 * BLOCK, LANES), BF16),
            pltpu.VMEM((SB_PAIRS, nkb, 2 * BLOCK, LANES), BF16),
            pltpu.VMEM((SB_PAIRS, SB_TILE, 2 * LANES), F32),
            pltpu.VMEM((SB_TILE, width), F32),
        ],
        compiler_params=_params("parallel", "parallel", "arbitrary"),
        name="sb_attention",
    )(qkv, qkv, qkv)


def _ret_kernel(cd_ref, qkv_ref, g_ref, dm_ref, qd_ref, kd_ref, y_ref, state_ref):
    hh, dk, dv = RET_HEADS, RET_QK_DIM, RET_V_DIM

    @pl.when(pl.program_id(1) == 0)
    def _():
        state_ref[...] = jnp.zeros_like(state_ref)

    chains = [(slice(c * BLOCK, (c + 1) * BLOCK), h) for c in range(RET_STEP_CHUNKS) for h in range(hh)]
    carry = [None] * len(chains)
    value = lambda rows, h: qkv_ref[rows, 2 * hh * dk + h * dv:2 * hh * dk + (h + 1) * dv]

    def scores(i):
        rows, h = chains[i]
        q = qkv_ref[rows, h * dk:(h + 1) * dk]
        kf = qkv_ref[rows, (hh + h) * dk:(hh + h + 1) * dk].astype(F32) * (dk ** -0.5)
        s = lax.dot_general(q, kf.astype(BF16), (((1,), (1,)), ((), ())), preferred_element_type=F32) * dm_ref[h]
        qd = (q.astype(F32) * qd_ref[h]).astype(BF16)
        kd = (kf * kd_ref[h]).astype(BF16)
        carry[i] = (s.astype(BF16), qd, kd)

    def outputs(i):
        rows, h = chains[i]
        s, qd, kd = carry[i]
        v = value(rows, h)
        state = state_ref[h]
        carry[i] = jnp.dot(s, v, preferred_element_type=F32) + jnp.dot(qd, state.astype(BF16),
                                                                      preferred_element_type=F32)
        state_ref[h] = state * cd_ref[h] + lax.dot_general(kd, v, (((0,), (0,)), ((), ())),
                                                           preferred_element_type=F32)

    def gated(i):
        rows, h = chains[i]
        o = carry[i]
        o = o * lax.rsqrt(jnp.mean(o * o, axis=-1, keepdims=True) + RMS_EPS)
        gate = g_ref[rows, h * dv:(h + 1) * dv]
        y_ref[rows, h * dv:(h + 1) * dv] = (gate * (1.0 / (1.0 + jnp.exp(-gate))) * o).astype(y_ref.dtype)
        carry[i] = None

    _staggered((scores, outputs, gated), len(chains))


def _ret_decay_tables():
    c, h = BLOCK, RET_HEADS
    log_gamma = jnp.log1p(-jnp.exp2(-5.0 - jnp.arange(h, dtype=F32)))
    idx = jnp.arange(c)
    diff = idx[:, None] - idx[None, :]
    decay_mat = jnp.where(diff >= 0, jnp.exp(log_gamma[:, None, None] * jnp.maximum(diff, 0).astype(F32)), 0.0)
    q_decay = jnp.exp(log_gamma[:, None] * (idx + 1).astype(F32))[:, :, None]
    k_decay = jnp.exp(log_gamma[:, None] * (c - 1 - idx).astype(F32))[:, :, None]
    chunk_decay = jnp.exp(log_gamma * c)
    return decay_mat, q_decay, k_decay, chunk_decay


def retention(qkv, gate, *, batch, seq):
    rows = RET_STEP_CHUNKS * BLOCK
    nc = seq // rows
    hh, dk, dv = RET_HEADS, RET_QK_DIM, RET_V_DIM
    dm, qd, kd, cd = _ret_decay_tables()
    return pl.pallas_call(
        _ret_kernel,
        out_shape=jax.ShapeDtypeStruct((batch * seq, hh * dv), BF16),
        grid=(batch, nc),
        in_specs=[
            pl.BlockSpec(memory_space=pltpu.SMEM),
            pl.BlockSpec((rows, qkv.shape[1]), lambda b, n: (b * nc + n, 0)),
            pl.BlockSpec((rows, hh * dv), lambda b, n: (b * nc + n, 0)),
            _resident(dm.shape),
            _resident(qd.shape),
            _resident(kd.shape),
        ],
        out_specs=pl.BlockSpec((rows, hh * dv), lambda b, n: (b * nc + n, 0)),
        scratch_shapes=[pltpu.VMEM((hh, dk, dv), F32)],
        compiler_params=_params("parallel", "arbitrary"),
        name="retention",
    )(cd, qkv, gate, dm, qd, kd)


def kernel(x, attn_norm, mlp_norm, final_norm, swa_w_qkv, swa_sinks, swa_w_o, sb_w_qkv, sb_w_o, ret_w_in, ret_w_o,
           mlp_w_up, mlp_w_down):
    batch, seq, d = x.shape
    h = x.reshape(batch * seq, d)
    ret_qkv_dim = 2 * RET_HEADS * RET_QK_DIM + RET_HEADS * RET_V_DIM
    ret_w_in_b, mlp_w_up_b, mlp_w_down_b = ret_w_in.astype(BF16), mlp_w_up.astype(BF16), mlp_w_down.astype(BF16)
    w_o_b = [w.astype(BF16) for w in (swa_w_o, sb_w_o, ret_w_o)]
    for i in range(DEPTH):
        kind, j = i % N_MIXERS, i // N_MIXERS
        if kind == 0:
            qkv, = norm_proj(h, attn_norm[i], swa_w_qkv, j, [(0, SWA_QKV_COLS, BF16)],
                             prepare=_swa_prepare, prepared_cols=SWA_QKV_COLS)
            mix = swa_attention(qkv, swa_sinks[j], batch=batch, seq=seq)
        elif kind == 1:
            sb_cols = sb_w_qkv.shape[-1]
            qkv, = norm_proj(h, attn_norm[i], sb_w_qkv, j, [(0, sb_cols, BF16)],
                             prepare=_cast_weight, prepared_cols=sb_cols)
            mix = sb_attention(qkv, batch=batch, seq=seq)
        else:
            qkv, gate = norm_proj(h, attn_norm[i], ret_w_in_b, j,
                                  [(0, ret_qkv_dim, BF16), (ret_qkv_dim, ret_w_in.shape[-1], F32)])
            mix = retention(qkv, gate, batch=batch, seq=seq)
        h = layer_tail(mix, w_o_b[kind], j, h, mlp_norm[i], mlp_w_up_b, mlp_w_down_b, i,
                       final_gain=final_norm if i == DEPTH - 1 else None)
    return h.reshape(batch, seq, d)
```

```python
import functools

import jax
import jax.numpy as jnp
from jax import lax
from jax.experimental import pallas as pl
from jax.experimental.pallas import tpu as pltpu

D_MODEL = 1024
DEPTH = 4
N_MIXERS = 3
RMS_EPS = 1e-6
BLOCK = 128
NEG = -1e30

SWA_HEAD_DIM = 64
SWA_HEADS = 16
SWA_KV_HEADS = 2
SWA_GROUP = 8
SB_HEAD_DIM = 64
SB_HEADS = 16
SB_TILE = 256
SB_PAIRS = 4
SB_EXP_ZERO = 104.0
RET_QK_DIM = 256
RET_HEADS = 4
RET_V_DIM = 512
RET_STEP_CHUNKS = 4
D_FF = 4 * D_MODEL

LANES = 128
VMEM_LIMIT = 48 * 1024 * 1024
TOKEN_TILE = 512
COL_CHUNK = 1024
LOG2E = 1.4426950408889634

F32 = jnp.float32
BF16 = jnp.bfloat16


def _params(*sem):
    return pltpu.CompilerParams(dimension_semantics=sem, vmem_limit_bytes=VMEM_LIMIT)


def _rms(x, g):
    return x * lax.rsqrt(jnp.mean(x * x, axis=-1, keepdims=True) + RMS_EPS) * g


def _staggered(stages, n):
    for tick in range(n + len(stages) - 1):
        for depth, stage in enumerate(stages):
            if 0 <= tick - depth < n:
                stage(tick - depth)


def _resident(shape):
    return pl.BlockSpec(shape, lambda *_: (0,) * len(shape), pipeline_mode=pl.Buffered(1))


def _resident_layer(stack, layer):
    return pl.BlockSpec((None,) + stack.shape[1:], lambda *_: (layer, 0, 0), pipeline_mode=pl.Buffered(1))


def _norm_proj_kernel(outs, prepare, x_ref, g_ref, w_ref, *refs):
    o_refs = refs[:len(outs)]
    if prepare is not None:
        wb_ref = refs[len(outs)]

        @pl.when(pl.program_id(0) == 0)
        def _():
            prepare(w_ref, wb_ref)
    else:
        wb_ref = w_ref
    xn = _rms(x_ref[...], g_ref[...]).astype(BF16)
    for (c0, c1, _), o_ref in zip(outs, o_refs):
        for c in range(c0, c1, COL_CHUNK):
            ce = min(c + COL_CHUNK, c1)
            o_ref[:, c - c0:ce - c0] = jnp.dot(xn, wb_ref[:, c:ce], preferred_element_type=F32).astype(o_ref.dtype)


def norm_proj(x, g, w_stack, layer, outs, prepare=None, prepared_cols=None):
    n, d = x.shape
    tm = TOKEN_TILE
    scratch = [] if prepare is None else [pltpu.VMEM((d, prepared_cols), BF16)]
    return pl.pallas_call(
        functools.partial(_norm_proj_kernel, tuple(outs), prepare),
        out_shape=[jax.ShapeDtypeStruct((n, c1 - c0), dt) for c0, c1, dt in outs],
        grid=(n // tm,),
        in_specs=[pl.BlockSpec((tm, d), lambda i: (i, 0)), _resident((1, d)), _resident_layer(w_stack, layer)],
        out_specs=[pl.BlockSpec((tm, c1 - c0), lambda i: (i, 0)) for c0, c1, _ in outs],
        scratch_shapes=scratch,
        compiler_params=_params("parallel" if prepare is None else "arbitrary"),
        name="norm_proj",
    )(x, g.reshape(1, d), w_stack)


def _cast_weight(w_ref, wb_ref):
    wb_ref[...] = w_ref[...].astype(BF16)


def _tail_kernel(final, mix_ref, wo_ref, h_ref, g_ref, wu_ref, wd_ref, *refs):
    if final:
        fg_ref, o_ref = refs
    else:
        o_ref, = refs
    h1 = h_ref[...] + jnp.dot(mix_ref[...], wo_ref[...], preferred_element_type=F32)
    xn = _rms(h1, g_ref[...]).astype(BF16)
    a = jnp.maximum(jnp.dot(xn, wu_ref[...], preferred_element_type=F32), 0.0)
    out = h1 + jnp.dot((a * a).astype(BF16), wd_ref[...], preferred_element_type=F32)
    o_ref[...] = _rms(out, fg_ref[...]) if final else out


def layer_tail(mix, w_o, o_layer, h, g, w_up, w_down, layer, final_gain=None):
    n, d = h.shape
    k = mix.shape[1]
    tm = TOKEN_TILE
    final = final_gain is not None
    in_specs = [
        pl.BlockSpec((tm, k), lambda i: (i, 0)),
        _resident_layer(w_o, o_layer),
        pl.BlockSpec((tm, d), lambda i: (i, 0)),
        _resident((1, d)),
        _resident_layer(w_up, layer),
        _resident_layer(w_down, layer),
    ]
    args = [mix, w_o, h, g.reshape(1, d), w_up, w_down]
    if final:
        in_specs.append(_resident((1, d)))
        args.append(final_gain.reshape(1, d))
    return pl.pallas_call(
        functools.partial(_tail_kernel, final),
        out_shape=jax.ShapeDtypeStruct((n, d), F32),
        grid=(n // tm,),
        in_specs=in_specs,
        out_specs=pl.BlockSpec((tm, d), lambda i: (i, 0)),
        compiler_params=_params("parallel"),
        name="layer_tail",
    )(*args)


def _swa_kernel(sink_ref, q_ref, kvp_ref, kvc_ref, o_ref):
    n = pl.program_id(1)
    q = q_ref[...] * jnp.asarray(SWA_HEAD_DIM ** -0.5, BF16)
    kv = jnp.concatenate([kvp_ref[...], kvc_ref[...]], axis=0)
    qi = lax.broadcasted_iota(jnp.int32, (BLOCK, BLOCK), 0)
    ci = lax.broadcasted_iota(jnp.int32, (BLOCK, BLOCK), 1)
    upper = ci > qi
    valid = jnp.logical_not(upper) | (n > 0)
    distf = jnp.where(upper, BLOCK + qi - ci, qi - ci).astype(F32)
    low = lax.broadcasted_iota(jnp.int32, (BLOCK, LANES), 1) < SWA_HEAD_DIM
    state = [None] * SWA_HEADS
    outs = [None] * SWA_HEADS

    def scores(h):
        g, c0 = h // SWA_GROUP, (h - h % 2) * SWA_HEAD_DIM
        q2 = q[:, c0:c0 + LANES]
        qm = jnp.where(low if h % 2 == 0 else jnp.logical_not(low), q2, jnp.zeros_like(q2))
        state[h] = lax.dot_general(qm, kv[:, g * LANES:(g + 1) * LANES], (((1,), (1,)), ((), ())),
                                   preferred_element_type=F32)

    def softmax(h):
        s = state[h]
        slope = 2.0 ** (-8.0 * (h + 1) / SWA_HEADS)
        s = jnp.where(upper, s[:, :BLOCK], s[:, BLOCK:])
        logits = jnp.where(valid, s - slope * distf, NEG)
        sink = sink_ref[h]
        m = jnp.maximum(jnp.max(logits, axis=-1, keepdims=True), sink)
        pexp = jnp.exp(logits - m)
        inv = 1.0 / (jnp.sum(pexp, axis=-1, keepdims=True) + jnp.exp(sink - m))
        pb = pexp.astype(BF16)
        zero = jnp.zeros_like(pb)
        state[h] = (jnp.concatenate([jnp.where(upper, pb, zero), jnp.where(upper, zero, pb)], axis=1), inv)

    def values(h):
        g, c0 = h // SWA_GROUP, (h - h % 2) * SWA_HEAD_DIM
        p2, inv = state[h]
        vv = kv[:, (SWA_KV_HEADS + g) * LANES:(SWA_KV_HEADS + g + 1) * LANES]
        outs[h] = jnp.dot(p2, vv, preferred_element_type=F32) * inv
        state[h] = None
        if h % 2 == 1:
            o_ref[:, c0:c0 + LANES] = jnp.where(low, outs[h - 1], outs[h]).astype(o_ref.dtype)
            outs[h - 1] = outs[h] = None

    _staggered((scores, softmax, values), SWA_HEADS)


def swa_attention(qkv, sinks, *, batch, seq):
    nb = seq // BLOCK
    dq = SWA_HEADS * SWA_HEAD_DIM
    dkv = 2 * SWA_KV_HEADS * LANES
    kv_col = dq // dkv
    return pl.pallas_call(
        _swa_kernel,
        out_shape=jax.ShapeDtypeStruct((batch * seq, dq), BF16),
        grid=(batch, nb),
        in_specs=[
            pl.BlockSpec(memory_space=pltpu.SMEM),
            pl.BlockSpec((BLOCK, dq), lambda b, n: (b * nb + n, 0)),
            pl.BlockSpec((BLOCK, dkv), lambda b, n: (b * nb + jnp.maximum(n - 1, 0), kv_col)),
            pl.BlockSpec((BLOCK, dkv), lambda b, n: (b * nb + n, kv_col)),
        ],
        out_specs=pl.BlockSpec((BLOCK, dq), lambda b, n: (b * nb + n, 0)),
        compiler_params=_params("parallel", "arbitrary"),
        name="swa_attention",
    )(sinks, qkv, qkv, qkv)


SWA_QKV_COLS = SWA_HEADS * SWA_HEAD_DIM + 2 * SWA_KV_HEADS * LANES


def _swa_prepare(w_ref, wb_ref):
    dq = SWA_HEADS * SWA_HEAD_DIM
    wb_ref[:, :dq] = w_ref[:, :dq].astype(BF16)
    low = lax.broadcasted_iota(jnp.int32, (w_ref.shape[0], LANES), 1) < SWA_HEAD_DIM
    for t in range(2):
        x = w_ref[:, dq + t * LANES:dq + (t + 1) * LANES]
        swapped = pltpu.roll(x, SWA_HEAD_DIM, axis=1)
        for g, dup in enumerate((jnp.where(low, x, swapped), jnp.where(low, swapped, x))):
            c = dq + (SWA_KV_HEADS * t + g) * LANES
            wb_ref[:, c:c + LANES] = dup.astype(BF16)


def _sb_kernel(q_ref, k_ref, v_ref, o_ref, kx_ref, vx_ref, run_ref, acc_ref):
    n = pl.program_id(2)
    nkb = k_ref.shape[0] // BLOCK
    sub = SB_TILE // BLOCK
    low = lax.broadcasted_iota(jnp.int32, (nkb, BLOCK, LANES), 2) < SB_HEAD_DIM

    @pl.when(n == 0)
    def _():
        for src, dst in ((k_ref, kx_ref), (v_ref, vx_ref)):
            for pp in range(SB_PAIRS):
                x = src[:, pp * LANES:(pp + 1) * LANES].reshape(nkb, BLOCK, LANES)
                zero = jnp.zeros_like(x)
                dst[pp, :, :BLOCK, :] = jnp.where(low, x, zero)
                dst[pp, :, BLOCK:, :] = jnp.where(low, zero, x)

    q = q_ref[...] * jnp.asarray(SB_HEAD_DIM ** -0.5, BF16)
    tj = lax.broadcasted_iota(jnp.int32, (2 * BLOCK, 2 * BLOCK), 0) % BLOCK
    ts = lax.broadcasted_iota(jnp.int32, (2 * BLOCK, 2 * BLOCK), 1)
    tri = jnp.where((tj > ts) | (ts >= BLOCK), 1.0, 0.0).astype(BF16)
    ti = lax.broadcasted_iota(jnp.int32, (BLOCK, 2 * BLOCK), 0)
    si = lax.broadcasted_iota(jnp.int32, (BLOCK, 2 * BLOCK), 1) % BLOCK
    causal = si < ti
    run_ref[...] = jnp.zeros_like(run_ref)
    acc_ref[...] = jnp.zeros_like(acc_ref)

    def mask_first_block(x):
        head = jnp.where(causal, x[:BLOCK], 0.0)
        return head if x.shape[0] == BLOCK else jnp.concatenate([head, x[BLOCK:]], axis=0)

    def slabs(specs):
        chains = [(pp, kb, r0) for kb, r0, _ in specs for pp in range(SB_PAIRS)]
        masked = [diag for _, _, diag in specs for pp in range(SB_PAIRS)]
        cols = lambda pp: slice(pp * LANES, (pp + 1) * LANES)
        state = [None] * len(chains)

        def scores(i):
            pp, kb, r0 = chains[i]
            state[i] = lax.dot_general(q[r0:, cols(pp)], kx_ref[pp, kb], (((1,), (1,)), ((), ())),
                                       preferred_element_type=F32)

        def prefix(i):
            z = state[i]
            t = jnp.maximum(z, 0.0) + jnp.log(1.0 + jnp.exp2(jnp.abs(z) * -LOG2E))
            log_beta = z - t
            if masked[i]:
                t = mask_first_block(t)
            hi = t.astype(BF16)
            lo = (t - hi.astype(F32)).astype(BF16)
            c = [jnp.dot(jnp.concatenate([hi[:, s:s + BLOCK], lo[:, s:s + BLOCK]], axis=1), tri,
                         preferred_element_type=F32) for s in (0, BLOCK)]
            state[i] = (log_beta, c)

        def values(i):
            pp, kb, r0 = chains[i]
            log_beta, c = state[i]
            cum = jnp.concatenate([c[0][:, :BLOCK], c[1][:, :BLOCK]], axis=1)
            tot = jnp.concatenate([c[0][:, BLOCK:], c[1][:, BLOCK:]], axis=1)
            run = run_ref[pp, r0:, :]
            a = jnp.exp(log_beta - cum - run)
            if masked[i]:
                a = mask_first_block(a)
            acc_ref[r0:, cols(pp)] += jnp.dot(a.astype(BF16), vx_ref[pp, kb], preferred_element_type=F32)
            run_ref[pp, r0:, :] = run + tot
            state[i] = None

        _staggered((scores, prefix, values), len(chains))

    diagonal = [(n * sub + j, j * BLOCK, True) for j in reversed(range(sub))]
    full = lambda tile_index: [(tile_index * sub + j, 0, False) for j in reversed(range(sub))]
    pl.when(n == 0)(lambda: slabs(diagonal))
    pl.when(n > 0)(lambda: slabs(diagonal + full(n - 1)))

    def tile(carry):
        i, _ = carry
        slabs(full(n - 1 - i))
        return i + 1, jnp.min(run_ref[...])

    lax.while_loop(lambda c: (c[0] < n) & (c[1] < SB_EXP_ZERO), tile, (jnp.int32(1), jnp.min(run_ref[...])))
    o_ref[...] = acc_ref[...].astype(o_ref.dtype)


def sb_attention(qkv, *, batch, seq):
    nt = seq // SB_TILE
    nkb = seq // BLOCK
    width = SB_PAIRS * LANES
    groups = SB_HEADS * SB_HEAD_DIM // width
    return pl.pallas_call(
        _sb_kernel,
        out_shape=jax.ShapeDtypeStruct((batch * seq, SB_HEADS * SB_HEAD_DIM), BF16),
        grid=(batch, groups, nt),
        in_specs=[
            pl.BlockSpec((SB_TILE, width), lambda b, p, n: (b * nt + n, p)),
            pl.BlockSpec((seq, width), lambda b, p, n: (b, groups + p)),
            pl.BlockSpec((seq, width), lambda b, p, n: (b, 2 * groups + p)),
        ],
        out_specs=pl.BlockSpec((SB_TILE, width), lambda b, p, n: (b * nt + n, p)),
        scratch_shapes=[
            pltpu.VMEM((SB_PAIRS, nkb, 2 * BLOCK, LANES), BF16),
            pltpu.VMEM((SB_PAIRS, nkb, 2 * BLOCK, LANES), BF16),
            pltpu.VMEM((SB_PAIRS, SB_TILE, 2 * LANES), F32),
            pltpu.VMEM((SB_TILE, width), F32),
        ],
        compiler_params=_params("parallel", "parallel", "arbitrary"),
        name="sb_attention",
    )(qkv, qkv, qkv)


def _ret_kernel(cd_ref, qkv_ref, g_ref, dm_ref, qd_ref, kd_ref, y_ref, state_ref):
    hh, dk, dv = RET_HEADS, RET_QK_DIM, RET_V_DIM

    @pl.when(pl.program_id(1) == 0)
    def _():
        state_ref[...] = jnp.zeros_like(state_ref)

    chains = [(slice(c * BLOCK, (c + 1) * BLOCK), h) for c in range(RET_STEP_CHUNKS) for h in range(hh)]
    carry = [None] * len(chains)
    value = lambda rows, h: qkv_ref[rows, 2 * hh * dk + h * dv:2 * hh * dk + (h + 1) * dv]

    def scores(i):
        rows, h = chains[i]
        q = qkv_ref[rows, h * dk:(h + 1) * dk]
        kf = qkv_ref[rows, (hh + h) * dk:(hh + h + 1) * dk].astype(F32) * (dk ** -0.5)
        s = lax.dot_general(q, kf.astype(BF16), (((1,), (1,)), ((), ())), preferred_element_type=F32) * dm_ref[h]
        qd = (q.astype(F32) * qd_ref[h]).astype(BF16)
        kd = (kf * kd_ref[h]).astype(BF16)
        carry[i] = (s.astype(BF16), qd, kd)

    def outputs(i):
        rows, h = chains[i]
        s, qd, kd = carry[i]
        v = value(rows, h)
        state = state_ref[h]
        carry[i] = jnp.dot(s, v, preferred_element_type=F32) + jnp.dot(qd, state.astype(BF16),
                                                                      preferred_element_type=F32)
        state_ref[h] = state * cd_ref[h] + lax.dot_general(kd, v, (((0,), (0,)), ((), ())),
                                                           preferred_element_type=F32)

    def gated(i):
        rows, h = chains[i]
        o = carry[i]
        o = o * lax.rsqrt(jnp.mean(o * o, axis=-1, keepdims=True) + RMS_EPS)
        gate = g_ref[rows, h * dv:(h + 1) * dv]
        y_ref[rows, h * dv:(h + 1) * dv] = (gate * (1.0 / (1.0 + jnp.exp(-gate))) * o).astype(y_ref.dtype)
        carry[i] = None

    _staggered((scores, outputs, gated), len(chains))


def _ret_decay_tables():
    c, h = BLOCK, RET_HEADS
    log_gamma = jnp.log1p(-jnp.exp2(-5.0 - jnp.arange(h, dtype=F32)))
    idx = jnp.arange(c)
    diff = idx[:, None] - idx[None, :]
    decay_mat = jnp.where(diff >= 0, jnp.exp(log_gamma[:, None, None] * jnp.maximum(diff, 0).astype(F32)), 0.0)
    q_decay = jnp.exp(log_gamma[:, None] * (idx + 1).astype(F32))[:, :, None]
    k_decay = jnp.exp(log_gamma[:, None] * (c - 1 - idx).astype(F32))[:, :, None]
    chunk_decay = jnp.exp(log_gamma * c)
    return decay_mat, q_decay, k_decay, chunk_decay


def retention(qkv, gate, *, batch, seq):
    rows = RET_STEP_CHUNKS * BLOCK
    nc = seq // rows
    hh, dk, dv = RET_HEADS, RET_QK_DIM, RET_V_DIM
    dm, qd, kd, cd = _ret_decay_tables()
    return pl.pallas_call(
        _ret_kernel,
        out_shape=jax.ShapeDtypeStruct((batch * seq, hh * dv), BF16),
        grid=(batch, nc),
        in_specs=[
            pl.BlockSpec(memory_space=pltpu.SMEM),
            pl.BlockSpec((rows, qkv.shape[1]), lambda b, n: (b * nc + n, 0)),
            pl.BlockSpec((rows, hh * dv), lambda b, n: (b * nc + n, 0)),
            _resident(dm.shape),
            _resident(qd.shape),
            _resident(kd.shape),
        ],
        out_specs=pl.BlockSpec((rows, hh * dv), lambda b, n: (b * nc + n, 0)),
        scratch_shapes=[pltpu.VMEM((hh, dk, dv), F32)],
        compiler_params=_params("parallel", "arbitrary"),
        name="retention",
    )(cd, qkv, gate, dm, qd, kd)


def kernel(x, attn_norm, mlp_norm, final_norm, swa_w_qkv, swa_sinks, swa_w_o, sb_w_qkv, sb_w_o, ret_w_in, ret_w_o,
           mlp_w_up, mlp_w_down):
    batch, seq, d = x.shape
    h = x.reshape(batch * seq, d)
    ret_qkv_dim = 2 * RET_HEADS * RET_QK_DIM + RET_HEADS * RET_V_DIM
    ret_w_in_b, mlp_w_up_b, mlp_w_down_b = ret_w_in.astype(BF16), mlp_w_up.astype(BF16), mlp_w_down.astype(BF16)
    w_o_b = [w.astype(BF16) for w in (swa_w_o, sb_w_o, ret_w_o)]
    for i in range(DEPTH):
        kind, j = i % N_MIXERS, i // N_MIXERS
        if kind == 0:
            qkv, = norm_proj(h, attn_norm[i], swa_w_qkv, j, [(0, SWA_QKV_COLS, BF16)],
                             prepare=_swa_prepare, prepared_cols=SWA_QKV_COLS)
            mix = swa_attention(qkv, swa_sinks[j], batch=batch, seq=seq)
        elif kind == 1:
            sb_cols = sb_w_qkv.shape[-1]
            qkv, = norm_proj(h, attn_norm[i], sb_w_qkv, j, [(0, sb_cols, BF16)],
                             prepare=_cast_weight, prepared_cols=sb_cols)
            mix = sb_attention(qkv, batch=batch, seq=seq)
        else:
            qkv, gate = norm_proj(h, attn_norm[i], ret_w_in_b, j,
                                  [(0, ret_qkv_dim, BF16), (ret_qkv_dim, ret_w_in.shape[-1], F32)])
            mix = retention(qkv, gate, batch=batch, seq=seq)
        h = layer_tail(mix, w_o_b[kind], j, h, mlp_norm[i], mlp_w_up_b, mlp_w_down_b, i,
                       final_gain=final_norm if i == DEPTH - 1 else None)
    return h.reshape(batch, seq, d)
```

```python
import functools

import jax
import jax.numpy as jnp
from jax import lax
from jax.experimental import pallas as pl
from jax.experimental.pallas import tpu as pltpu

D_MODEL = 1024
DEPTH = 4
N_MIXERS = 3
RMS_EPS = 1e-6
BLOCK = 128
NEG = -1e30

SWA_HEAD_DIM = 64
SWA_HEADS = 16
SWA_KV_HEADS = 2
SWA_GROUP = 8
SWA_STEP_BLOCKS = 4
SB_HEAD_DIM = 64
SB_HEADS = 16
SB_TILE = 256
SB_PAIRS = 4
SB_EXP_ZERO = 104.0
RET_QK_DIM = 256
RET_HEADS = 4
RET_V_DIM = 512
RET_STEP_CHUNKS = 4
D_FF = 4 * D_MODEL

LANES = 128
VMEM_LIMIT = 48 * 1024 * 1024
TOKEN_TILE = 512
COL_CHUNK = 1024
LOG2E = 1.4426950408889634

F32 = jnp.float32
BF16 = jnp.bfloat16


def _params(*sem):
    return pltpu.CompilerParams(dimension_semantics=sem, vmem_limit_bytes=VMEM_LIMIT)


def _rms(x, g):
    return x * lax.rsqrt(jnp.mean(x * x, axis=-1, keepdims=True) + RMS_EPS) * g


def _staggered(stages, n):
    for tick in range(n + len(stages) - 1):
        for depth, stage in enumerate(stages):
            if 0 <= tick - depth < n:
                stage(tick - depth)


def _resident(shape):
    return pl.BlockSpec(shape, lambda *_: (0,) * len(shape), pipeline_mode=pl.Buffered(1))


def _resident_layer(stack, layer):
    return pl.BlockSpec((None,) + stack.shape[1:], lambda *_: (layer, 0, 0), pipeline_mode=pl.Buffered(1))


def _norm_proj_kernel(outs, prepare, x_ref, g_ref, w_ref, *refs):
    o_refs = refs[:len(outs)]
    if prepare is not None:
        wb_ref = refs[len(outs)]

        @pl.when(pl.program_id(0) == 0)
        def _():
            prepare(w_ref, wb_ref)
    else:
        wb_ref = w_ref
    xn = _rms(x_ref[...], g_ref[...]).astype(BF16)
    for (c0, c1, _), o_ref in zip(outs, o_refs):
        for c in range(c0, c1, COL_CHUNK):
            ce = min(c + COL_CHUNK, c1)
            o_ref[:, c - c0:ce - c0] = jnp.dot(xn, wb_ref[:, c:ce], preferred_element_type=F32).astype(o_ref.dtype)


def norm_proj(x, g, w_stack, layer, outs, prepare=None, prepared_cols=None):
    n, d = x.shape
    tm = TOKEN_TILE
    scratch = [] if prepare is None else [pltpu.VMEM((d, prepared_cols), BF16)]
    return pl.pallas_call(
        functools.partial(_norm_proj_kernel, tuple(outs), prepare),
        out_shape=[jax.ShapeDtypeStruct((n, c1 - c0), dt) for c0, c1, dt in outs],
        grid=(n // tm,),
        in_specs=[pl.BlockSpec((tm, d), lambda i: (i, 0)), _resident((1, d)), _resident_layer(w_stack, layer)],
        out_specs=[pl.BlockSpec((tm, c1 - c0), lambda i: (i, 0)) for c0, c1, _ in outs],
        scratch_shapes=scratch,
        compiler_params=_params("parallel" if prepare is None else "arbitrary"),
        name="norm_proj",
    )(x, g.reshape(1, d), w_stack)


def _cast_weight(w_ref, wb_ref):
    wb_ref[...] = w_ref[...].astype(BF16)


def _tail_kernel(final, mix_ref, wo_ref, h_ref, g_ref, wu_ref, wd_ref, *refs):
    if final:
        fg_ref, o_ref = refs
    else:
        o_ref, = refs
    h1 = h_ref[...] + jnp.dot(mix_ref[...], wo_ref[...], preferred_element_type=F32)
    xn = _rms(h1, g_ref[...]).astype(BF16)
    a = jnp.maximum(jnp.dot(xn, wu_ref[...], preferred_element_type=F32), 0.0)
    out = h1 + jnp.dot((a * a).astype(BF16), wd_ref[...], preferred_element_type=F32)
    o_ref[...] = _rms(out, fg_ref[...]) if final else out


def layer_tail(mix, w_o, o_layer, h, g, w_up, w_down, layer, final_gain=None):
    n, d = h.shape
    k = mix.shape[1]
    tm = TOKEN_TILE
    final = final_gain is not None
    in_specs = [
        pl.BlockSpec((tm, k), lambda i: (i, 0)),
        _resident_layer(w_o, o_layer),
        pl.BlockSpec((tm, d), lambda i: (i, 0)),
        _resident((1, d)),
        _resident_layer(w_up, layer),
        _resident_layer(w_down, layer),
    ]
    args = [mix, w_o, h, g.reshape(1, d), w_up, w_down]
    if final:
        in_specs.append(_resident((1, d)))
        args.append(final_gain.reshape(1, d))
    return pl.pallas_call(
        functools.partial(_tail_kernel, final),
        out_shape=jax.ShapeDtypeStruct((n, d), F32),
        grid=(n // tm,),
        in_specs=in_specs,
        out_specs=pl.BlockSpec((tm, d), lambda i: (i, 0)),
        compiler_params=_params("parallel"),
        name="layer_tail",
    )(*args)


def _swa_kernel(sink_ref, q_ref, kvp_ref, kvc_ref, o_ref):
    n = pl.program_id(1)
    q = q_ref[...] * jnp.asarray(SWA_HEAD_DIM ** -0.5, BF16)
    kv = jnp.concatenate([kvp_ref[...], kvc_ref[...]], axis=0)
    qi = lax.broadcasted_iota(jnp.int32, (BLOCK, BLOCK), 0)
    ci = lax.broadcasted_iota(jnp.int32, (BLOCK, BLOCK), 1)
    upper = ci > qi
    valid = jnp.logical_not(upper) | (n > 0)
    distf = jnp.where(upper, BLOCK + qi - ci, qi - ci).astype(F32)
    low = lax.broadcasted_iota(jnp.int32, (BLOCK, LANES), 1) < SWA_HEAD_DIM
    chains = SWA_STEP_BLOCKS * SWA_HEADS
    state = [None] * chains
    outs = [None] * chains

    def scores(i):
        c, h = divmod(i, SWA_HEADS)
        g, c0 = h // SWA_GROUP, (h - h % 2) * SWA_HEAD_DIM
        q2 = q[c * BLOCK:(c + 1) * BLOCK, c0:c0 + LANES]
        qm = jnp.where(low if h % 2 == 0 else jnp.logical_not(low), q2, jnp.zeros_like(q2))
        state[i] = lax.dot_general(qm, kv[c * BLOCK:(c + 2) * BLOCK, g * LANES:(g + 1) * LANES],
                                   (((1,), (1,)), ((), ())), preferred_element_type=F32)

    def softmax(i):
        c, h = divmod(i, SWA_HEADS)
        s = state[i]
        slope = 2.0 ** (-8.0 * (h + 1) / SWA_HEADS)
        logits = jnp.where(upper, s[:, :BLOCK], s[:, BLOCK:]) - slope * distf
        if c == 0:
            logits = jnp.where(valid, logits, NEG)
        sink = sink_ref[h]
        m = jnp.maximum(jnp.max(logits, axis=-1, keepdims=True), sink)
        pexp = jnp.exp(logits - m)
        inv = 1.0 / (jnp.sum(pexp, axis=-1, keepdims=True) + jnp.exp(sink - m))
        pb = pexp.astype(BF16)
        zero = jnp.zeros_like(pb)
        state[i] = (jnp.concatenate([jnp.where(upper, pb, zero), jnp.where(upper, zero, pb)], axis=1), inv)

    def values(i):
        c, h = divmod(i, SWA_HEADS)
        g, c0 = h // SWA_GROUP, (h - h % 2) * SWA_HEAD_DIM
        p2, inv = state[i]
        vv = kv[c * BLOCK:(c + 2) * BLOCK, (SWA_KV_HEADS + g) * LANES:(SWA_KV_HEADS + g + 1) * LANES]
        outs[i] = jnp.dot(p2, vv, preferred_element_type=F32) * inv
        state[i] = None
        if h % 2 == 1:
            o_ref[c * BLOCK:(c + 1) * BLOCK, c0:c0 + LANES] = jnp.where(low, outs[i - 1], outs[i]).astype(o_ref.dtype)
            outs[i - 1] = outs[i] = None

    _staggered((scores, softmax, values), chains)


def swa_attention(qkv, sinks, *, batch, seq):
    rows = SWA_STEP_BLOCKS * BLOCK
    nb = seq // rows
    dq = SWA_HEADS * SWA_HEAD_DIM
    dkv = 2 * SWA_KV_HEADS * LANES
    kv_col = dq // dkv
    return pl.pallas_call(
        _swa_kernel,
        out_shape=jax.ShapeDtypeStruct((batch * seq, dq), BF16),
        grid=(batch, nb),
        in_specs=[
            pl.BlockSpec(memory_space=pltpu.SMEM),
            pl.BlockSpec((rows, dq), lambda b, n: (b * nb + n, 0)),
            pl.BlockSpec((BLOCK, dkv), lambda b, n: ((b * nb + n) * SWA_STEP_BLOCKS - jnp.minimum(n, 1), kv_col)),
            pl.BlockSpec((rows, dkv), lambda b, n: (b * nb + n, kv_col)),
        ],
        out_specs=pl.BlockSpec((rows, dq), lambda b, n: (b * nb + n, 0)),
        compiler_params=_params("parallel", "arbitrary"),
        name="swa_attention",
    )(sinks, qkv, qkv, qkv)


SWA_QKV_COLS = SWA_HEADS * SWA_HEAD_DIM + 2 * SWA_KV_HEADS * LANES


def _swa_prepare(w_ref, wb_ref):
    dq = SWA_HEADS * SWA_HEAD_DIM
    wb_ref[:, :dq] = w_ref[:, :dq].astype(BF16)
    low = lax.broadcasted_iota(jnp.int32, (w_ref.shape[0], LANES), 1) < SWA_HEAD_DIM
    for t in range(2):
        x = w_ref[:, dq + t * LANES:dq + (t + 1) * LANES]
        swapped = pltpu.roll(x, SWA_HEAD_DIM, axis=1)
        for g, dup in enumerate((jnp.where(low, x, swapped), jnp.where(low, swapped, x))):
            c = dq + (SWA_KV_HEADS * t + g) * LANES
            wb_ref[:, c:c + LANES] = dup.astype(BF16)


def _sb_kernel(q_ref, k_ref, v_ref, o_ref, kx_ref, vx_ref, run_ref, acc_ref):
    n = pl.program_id(2)
    nkb = k_ref.shape[0] // BLOCK
    sub = SB_TILE // BLOCK
    low = lax.broadcasted_iota(jnp.int32, (nkb, BLOCK, LANES), 2) < SB_HEAD_DIM

    @pl.when(n == 0)
    def _():
        for src, dst in ((k_ref, kx_ref), (v_ref, vx_ref)):
            for pp in range(SB_PAIRS):
                x = src[:, pp * LANES:(pp + 1) * LANES].reshape(nkb, BLOCK, LANES)
                zero = jnp.zeros_like(x)
                dst[pp, :, :BLOCK, :] = jnp.where(low, x, zero)
                dst[pp, :, BLOCK:, :] = jnp.where(low, zero, x)

    q = q_ref[...] * jnp.asarray(SB_HEAD_DIM ** -0.5, BF16)
    tj = lax.broadcasted_iota(jnp.int32, (2 * BLOCK, 2 * BLOCK), 0) % BLOCK
    ts = lax.broadcasted_iota(jnp.int32, (2 * BLOCK, 2 * BLOCK), 1)
    tri = jnp.where((tj > ts) | (ts >= BLOCK), 1.0, 0.0).astype(BF16)
    ti = lax.broadcasted_iota(jnp.int32, (BLOCK, 2 * BLOCK), 0)
    si = lax.broadcasted_iota(jnp.int32, (BLOCK, 2 * BLOCK), 1) % BLOCK
    causal = si < ti
    run_ref[...] = jnp.zeros_like(run_ref)
    acc_ref[...] = jnp.zeros_like(acc_ref)

    def mask_first_block(x):
        head = jnp.where(causal, x[:BLOCK], 0.0)
        return head if x.shape[0] == BLOCK else jnp.concatenate([head, x[BLOCK:]], axis=0)

    def slabs(specs):
        chains = [(pp, kb, r0) for kb, r0, _ in specs for pp in range(SB_PAIRS)]
        masked = [diag for _, _, diag in specs for pp in range(SB_PAIRS)]
        cols = lambda pp: slice(pp * LANES, (pp + 1) * LANES)
        state = [None] * len(chains)

        def scores(i):
            pp, kb, r0 = chains[i]
            state[i] = lax.dot_general(q[r0:, cols(pp)], kx_ref[pp, kb], (((1,), (1,)), ((), ())),
                                       preferred_element_type=F32)

        def prefix(i):
            z = state[i]
            t = jnp.maximum(z, 0.0) + jnp.log(1.0 + jnp.exp2(jnp.abs(z) * -LOG2E))
            log_beta = z - t
            if masked[i]:
                t = mask_first_block(t)
            hi = t.astype(BF16)
            lo = (t - hi.astype(F32)).astype(BF16)
            c = [jnp.dot(jnp.concatenate([hi[:, s:s + BLOCK], lo[:, s:s + BLOCK]], axis=1), tri,
                         preferred_element_type=F32) for s in (0, BLOCK)]
            state[i] = (log_beta, c)

        def values(i):
            pp, kb, r0 = chains[i]
            log_beta, c = state[i]
            cum = jnp.concatenate([c[0][:, :BLOCK], c[1][:, :BLOCK]], axis=1)
            tot = jnp.concatenate([c[0][:, BLOCK:], c[1][:, BLOCK:]], axis=1)
            run = run_ref[pp, r0:, :]
            a = jnp.exp(log_beta - cum - run)
            if masked[i]:
                a = mask_first_block(a)
            acc_ref[r0:, cols(pp)] += jnp.dot(a.astype(BF16), vx_ref[pp, kb], preferred_element_type=F32)
            run_ref[pp, r0:, :] = run + tot
            state[i] = None

        _staggered((scores, prefix, values), len(chains))

    slabs([(n * sub + j, j * BLOCK, True) for j in reversed(range(sub))])

    def tile(carry):
        i, _ = carry
        kb0 = (n - 1 - i) * sub
        slabs([(kb0 + j, 0, False) for j in reversed(range(sub))])
        return i + 1, jnp.min(run_ref[...])

    lax.while_loop(lambda c: (c[0] < n) & (c[1] < SB_EXP_ZERO), tile, (jnp.int32(0), jnp.float32(0.0)))
    o_ref[...] = acc_ref[...].astype(o_ref.dtype)


def sb_attention(qkv, *, batch, seq):
    nt = seq // SB_TILE
    nkb = seq // BLOCK
    width = SB_PAIRS * LANES
    groups = SB_HEADS * SB_HEAD_DIM // width
    return pl.pallas_call(
        _sb_kernel,
        out_shape=jax.ShapeDtypeStruct((batch * seq, SB_HEADS * SB_HEAD_DIM), BF16),
        grid=(batch, groups, nt),
        in_specs=[
            pl.BlockSpec((SB_TILE, width), lambda b, p, n: (b * nt + n, p)),
            pl.BlockSpec((seq, width), lambda b, p, n: (b, groups + p)),
            pl.BlockSpec((seq, width), lambda b, p, n: (b, 2 * groups + p)),
        ],
        out_specs=pl.BlockSpec((SB_TILE, width), lambda b, p, n: (b * nt + n, p)),
        scratch_shapes=[
            pltpu.VMEM((SB_PAIRS, nkb, 2 * BLOCK, LANES), BF16),
            pltpu.VMEM((SB_PAIRS, nkb, 2 * BLOCK, LANES), BF16),
            pltpu.VMEM((SB_PAIRS, SB_TILE, 2 * LANES), F32),
            pltpu.VMEM((SB_TILE, width), F32),
        ],
        compiler_params=_params("parallel", "parallel", "arbitrary"),
        name="sb_attention",
    )(qkv, qkv, qkv)


def _ret_kernel(cd_ref, qkv_ref, g_ref, dm_ref, qd_ref, kd_ref, y_ref, state_ref):
    hh, dk, dv = RET_HEADS, RET_QK_DIM, RET_V_DIM

    @pl.when(pl.program_id(1) == 0)
    def _():
        state_ref[...] = jnp.zeros_like(state_ref)

    chains = [(slice(c * BLOCK, (c + 1) * BLOCK), h) for c in range(RET_STEP_CHUNKS) for h in range(hh)]
    carry = [None] * len(chains)
    value = lambda rows, h: qkv_ref[rows, 2 * hh * dk + h * dv:2 * hh * dk + (h + 1) * dv]

    def scores(i):
        rows, h = chains[i]
        q = qkv_ref[rows, h * dk:(h + 1) * dk]
        kf = qkv_ref[rows, (hh + h) * dk:(hh + h + 1) * dk].astype(F32) * (dk ** -0.5)
        s = lax.dot_general(q, kf.astype(BF16), (((1,), (1,)), ((), ())), preferred_element_type=F32) * dm_ref[h]
        qd = (q.astype(F32) * qd_ref[h]).astype(BF16)
        kd = (kf * kd_ref[h]).astype(BF16)
        carry[i] = (s.astype(BF16), qd, kd)

    def outputs(i):
        rows, h = chains[i]
        s, qd, kd = carry[i]
        v = value(rows, h)
        state = state_ref[h]
        carry[i] = jnp.dot(s, v, preferred_element_type=F32) + jnp.dot(qd, state.astype(BF16),
                                                                      preferred_element_type=F32)
        state_ref[h] = state * cd_ref[h] + lax.dot_general(kd, v, (((0,), (0,)), ((), ())),
                                                           preferred_element_type=F32)

    def gated(i):
        rows, h = chains[i]
        o = carry[i]
        o = o * lax.rsqrt(jnp.mean(o * o, axis=-1, keepdims=True) + RMS_EPS)
        gate = g_ref[rows, h * dv:(h + 1) * dv]
        y_ref[rows, h * dv:(h + 1) * dv] = (gate * (1.0 / (1.0 + jnp.exp(-gate))) * o).astype(y_ref.dtype)
        carry[i] = None

    _staggered((scores, outputs, gated), len(chains))


def _ret_decay_tables():
    c, h = BLOCK, RET_HEADS
    log_gamma = jnp.log1p(-jnp.exp2(-5.0 - jnp.arange(h, dtype=F32)))
    idx = jnp.arange(c)
    diff = idx[:, None] - idx[None, :]
    decay_mat = jnp.where(diff >= 0, jnp.exp(log_gamma[:, None, None] * jnp.maximum(diff, 0).astype(F32)), 0.0)
    q_decay = jnp.exp(log_gamma[:, None] * (idx + 1).astype(F32))[:, :, None]
    k_decay = jnp.exp(log_gamma[:, None] * (c - 1 - idx).astype(F32))[:, :, None]
    chunk_decay = jnp.exp(log_gamma * c)
    return decay_mat, q_decay, k_decay, chunk_decay


def retention(qkv, gate, *, batch, seq):
    rows = RET_STEP_CHUNKS * BLOCK
    nc = seq // rows
    hh, dk, dv = RET_HEADS, RET_QK_DIM, RET_V_DIM
    dm, qd, kd, cd = _ret_decay_tables()
    return pl.pallas_call(
        _ret_kernel,
        out_shape=jax.ShapeDtypeStruct((batch * seq, hh * dv), BF16),
        grid=(batch, nc),
        in_specs=[
            pl.BlockSpec(memory_space=pltpu.SMEM),
            pl.BlockSpec((rows, qkv.shape[1]), lambda b, n: (b * nc + n, 0)),
            pl.BlockSpec((rows, hh * dv), lambda b, n: (b * nc + n, 0)),
            _resident(dm.shape),
            _resident(qd.shape),
            _resident(kd.shape),
        ],
        out_specs=pl.BlockSpec((rows, hh * dv), lambda b, n: (b * nc + n, 0)),
        scratch_shapes=[pltpu.VMEM((hh, dk, dv), F32)],
        compiler_params=_params("parallel", "arbitrary"),
        name="retention",
    )(cd, qkv, gate, dm, qd, kd)


def kernel(x, attn_norm, mlp_norm, final_norm, swa_w_qkv, swa_sinks, swa_w_o, sb_w_qkv, sb_w_o, ret_w_in, ret_w_o,
           mlp_w_up, mlp_w_down):
    batch, seq, d = x.shape
    h = x.reshape(batch * seq, d)
    ret_qkv_dim = 2 * RET_HEADS * RET_QK_DIM + RET_HEADS * RET_V_DIM
    ret_w_in_b, mlp_w_up_b, mlp_w_down_b = ret_w_in.astype(BF16), mlp_w_up.astype(BF16), mlp_w_down.astype(BF16)
    w_o_b = [w.astype(BF16) for w in (swa_w_o, sb_w_o, ret_w_o)]
    for i in range(DEPTH):
        kind, j = i % N_MIXERS, i // N_MIXERS
        if kind == 0:
            qkv, = norm_proj(h, attn_norm[i], swa_w_qkv, j, [(0, SWA_QKV_COLS, BF16)],
                             prepare=_swa_prepare, prepared_cols=SWA_QKV_COLS)
            mix = swa_attention(qkv, swa_sinks[j], batch=batch, seq=seq)
        elif kind == 1:
            sb_cols = sb_w_qkv.shape[-1]
            qkv, = norm_proj(h, attn_norm[i], sb_w_qkv, j, [(0, sb_cols, BF16)],
                             prepare=_cast_weight, prepared_cols=sb_cols)
            mix = sb_attention(qkv, batch=batch, seq=seq)
        else:
            qkv, gate = norm_proj(h, attn_norm[i], ret_w_in_b, j,
                                  [(0, ret_qkv_dim, BF16), (ret_qkv_dim, ret_w_in.shape[-1], F32)])
            mix = retention(qkv, gate, batch=batch, seq=seq)
        h = layer_tail(mix, w_o_b[kind], j, h, mlp_norm[i], mlp_w_up_b, mlp_w_down_b, i,
                       final_gain=final_norm if i == DEPTH - 1 else None)
    return h.reshape(batch, seq, d)
```

```python
import functools

import jax
import jax.numpy as jnp
from jax import lax
from jax.experimental import pallas as pl
from jax.experimental.pallas import tpu as pltpu

D_MODEL = 1024
DEPTH = 4
N_MIXERS = 3
RMS_EPS = 1e-6
BLOCK = 128
NEG = -1e30

SWA_HEAD_DIM = 64
SWA_HEADS = 16
SWA_KV_HEADS = 2
SWA_GROUP = 8
SWA_STEP_BLOCKS = 8
SB_HEAD_DIM = 64
SB_HEADS = 16
SB_TILE = 256
SB_PAIRS = 4
SB_EXP_ZERO = 104.0
RET_QK_DIM = 256
RET_HEADS = 4
RET_V_DIM = 512
RET_STEP_CHUNKS = 4
D_FF = 4 * D_MODEL

LANES = 128
VMEM_LIMIT = 48 * 1024 * 1024
TOKEN_TILE = 512
COL_CHUNK = 1024
LOG2E = 1.4426950408889634

F32 = jnp.float32
BF16 = jnp.bfloat16


def _params(*sem):
    return pltpu.CompilerParams(dimension_semantics=sem, vmem_limit_bytes=VMEM_LIMIT)


def _rms(x, g):
    return x * lax.rsqrt(jnp.mean(x * x, axis=-1, keepdims=True) + RMS_EPS) * g


def _staggered(stages, n):
    for tick in range(n + len(stages) - 1):
        for depth, stage in enumerate(stages):
            if 0 <= tick - depth < n:
                stage(tick - depth)


def _resident(shape):
    return pl.BlockSpec(shape, lambda *_: (0,) * len(shape), pipeline_mode=pl.Buffered(1))


def _resident_layer(stack, layer):
    return pl.BlockSpec((None,) + stack.shape[1:], lambda *_: (layer, 0, 0), pipeline_mode=pl.Buffered(1))


def _norm_proj_kernel(outs, prepare, x_ref, g_ref, w_ref, *refs):
    o_refs = refs[:len(outs)]
    if prepare is not None:
        wb_ref = refs[len(outs)]

        @pl.when(pl.program_id(0) == 0)
        def _():
            prepare(w_ref, wb_ref)
    else:
        wb_ref = w_ref
    xn = _rms(x_ref[...], g_ref[...]).astype(BF16)
    for (c0, c1, _), o_ref in zip(outs, o_refs):
        for c in range(c0, c1, COL_CHUNK):
            ce = min(c + COL_CHUNK, c1)
            o_ref[:, c - c0:ce - c0] = jnp.dot(xn, wb_ref[:, c:ce], preferred_element_type=F32).astype(o_ref.dtype)


def norm_proj(x, g, w_stack, layer, outs, prepare=None, prepared_cols=None):
    n, d = x.shape
    tm = TOKEN_TILE
    scratch = [] if prepare is None else [pltpu.VMEM((d, prepared_cols), BF16)]
    return pl.pallas_call(
        functools.partial(_norm_proj_kernel, tuple(outs), prepare),
        out_shape=[jax.ShapeDtypeStruct((n, c1 - c0), dt) for c0, c1, dt in outs],
        grid=(n // tm,),
        in_specs=[pl.BlockSpec((tm, d), lambda i: (i, 0)), _resident((1, d)), _resident_layer(w_stack, layer)],
        out_specs=[pl.BlockSpec((tm, c1 - c0), lambda i: (i, 0)) for c0, c1, _ in outs],
        scratch_shapes=scratch,
        compiler_params=_params("parallel" if prepare is None else "arbitrary"),
        name="norm_proj",
    )(x, g.reshape(1, d), w_stack)


def _cast_weight(w_ref, wb_ref):
    wb_ref[...] = w_ref[...].astype(BF16)


def _tail_kernel(final, mix_ref, wo_ref, h_ref, g_ref, wu_ref, wd_ref, *refs):
    if final:
        fg_ref, o_ref = refs
    else:
        o_ref, = refs
    h1 = h_ref[...] + jnp.dot(mix_ref[...], wo_ref[...], preferred_element_type=F32)
    xn = _rms(h1, g_ref[...]).astype(BF16)
    a = jnp.maximum(jnp.dot(xn, wu_ref[...], preferred_element_type=F32), 0.0)
    out = h1 + jnp.dot((a * a).astype(BF16), wd_ref[...], preferred_element_type=F32)
    o_ref[...] = _rms(out, fg_ref[...]) if final else out


def layer_tail(mix, w_o, o_layer, h, g, w_up, w_down, layer, final_gain=None):
    n, d = h.shape
    k = mix.shape[1]
    tm = TOKEN_TILE
    final = final_gain is not None
    in_specs = [
        pl.BlockSpec((tm, k), lambda i: (i, 0)),
        _resident_layer(w_o, o_layer),
        pl.BlockSpec((tm, d), lambda i: (i, 0)),
        _resident((1, d)),
        _resident_layer(w_up, layer),
        _resident_layer(w_down, layer),
    ]
    args = [mix, w_o, h, g.reshape(1, d), w_up, w_down]
    if final:
        in_specs.append(_resident((1, d)))
        args.append(final_gain.reshape(1, d))
    return pl.pallas_call(
        functools.partial(_tail_kernel, final),
        out_shape=jax.ShapeDtypeStruct((n, d), F32),
        grid=(n // tm,),
        in_specs=in_specs,
        out_specs=pl.BlockSpec((tm, d), lambda i: (i, 0)),
        compiler_params=_params("parallel"),
        name="layer_tail",
    )(*args)


def _swa_kernel(sink_ref, q_ref, kvp_ref, kvc_ref, o_ref):
    n = pl.program_id(1)
    q = q_ref[...] * jnp.asarray(SWA_HEAD_DIM ** -0.5, BF16)
    kv = jnp.concatenate([kvp_ref[...], kvc_ref[...]], axis=0)
    qi = lax.broadcasted_iota(jnp.int32, (BLOCK, BLOCK), 0)
    ci = lax.broadcasted_iota(jnp.int32, (BLOCK, BLOCK), 1)
    upper = ci > qi
    valid = jnp.logical_not(upper) | (n > 0)
    distf = jnp.where(upper, BLOCK + qi - ci, qi - ci).astype(F32)
    low = lax.broadcasted_iota(jnp.int32, (BLOCK, LANES), 1) < SWA_HEAD_DIM
    chains = SWA_STEP_BLOCKS * SWA_HEADS
    state = [None] * chains
    outs = [None] * chains

    def scores(i):
        c, h = divmod(i, SWA_HEADS)
        g, c0 = h // SWA_GROUP, (h - h % 2) * SWA_HEAD_DIM
        q2 = q[c * BLOCK:(c + 1) * BLOCK, c0:c0 + LANES]
        qm = jnp.where(low if h % 2 == 0 else jnp.logical_not(low), q2, jnp.zeros_like(q2))
        state[i] = lax.dot_general(qm, kv[c * BLOCK:(c + 2) * BLOCK, g * LANES:(g + 1) * LANES],
                                   (((1,), (1,)), ((), ())), preferred_element_type=F32)

    def softmax(i):
        c, h = divmod(i, SWA_HEADS)
        s = state[i]
        slope = 2.0 ** (-8.0 * (h + 1) / SWA_HEADS)
        logits = jnp.where(upper, s[:, :BLOCK], s[:, BLOCK:]) - slope * distf
        if c == 0:
            logits = jnp.where(valid, logits, NEG)
        sink = sink_ref[h]
        m = jnp.maximum(jnp.max(logits, axis=-1, keepdims=True), sink)
        pexp = jnp.exp(logits - m)
        inv = 1.0 / (jnp.sum(pexp, axis=-1, keepdims=True) + jnp.exp(sink - m))
        pb = pexp.astype(BF16)
        zero = jnp.zeros_like(pb)
        state[i] = (jnp.concatenate([jnp.where(upper, pb, zero), jnp.where(upper, zero, pb)], axis=1), inv)

    def values(i):
        c, h = divmod(i, SWA_HEADS)
        g, c0 = h // SWA_GROUP, (h - h % 2) * SWA_HEAD_DIM
        p2, inv = state[i]
        vv = kv[c * BLOCK:(c + 2) * BLOCK, (SWA_KV_HEADS + g) * LANES:(SWA_KV_HEADS + g + 1) * LANES]
        outs[i] = jnp.dot(p2, vv, preferred_element_type=F32) * inv
        state[i] = None
        if h % 2 == 1:
            o_ref[c * BLOCK:(c + 1) * BLOCK, c0:c0 + LANES] = jnp.where(low, outs[i - 1], outs[i]).astype(o_ref.dtype)
            outs[i - 1] = outs[i] = None

    _staggered((scores, softmax, values), chains)


def swa_attention(qkv, sinks, *, batch, seq):
    rows = SWA_STEP_BLOCKS * BLOCK
    nb = seq // rows
    dq = SWA_HEADS * SWA_HEAD_DIM
    dkv = 2 * SWA_KV_HEADS * LANES
    kv_col = dq // dkv
    return pl.pallas_call(
        _swa_kernel,
        out_shape=jax.ShapeDtypeStruct((batch * seq, dq), BF16),
        grid=(batch, nb),
        in_specs=[
            pl.BlockSpec(memory_space=pltpu.SMEM),
            pl.BlockSpec((rows, dq), lambda b, n: (b * nb + n, 0)),
            pl.BlockSpec((BLOCK, dkv), lambda b, n: ((b * nb + n) * SWA_STEP_BLOCKS - jnp.minimum(n, 1), kv_col)),
            pl.BlockSpec((rows, dkv), lambda b, n: (b * nb + n, kv_col)),
        ],
        out_specs=pl.BlockSpec((rows, dq), lambda b, n: (b * nb + n, 0)),
        compiler_params=_params("parallel", "arbitrary"),
        name="swa_attention",
    )(sinks, qkv, qkv, qkv)


SWA_QKV_COLS = SWA_HEADS * SWA_HEAD_DIM + 2 * SWA_KV_HEADS * LANES


def _swa_prepare(w_ref, wb_ref):
    dq = SWA_HEADS * SWA_HEAD_DIM
    wb_ref[:, :dq] = w_ref[:, :dq].astype(BF16)
    low = lax.broadcasted_iota(jnp.int32, (w_ref.shape[0], LANES), 1) < SWA_HEAD_DIM
    for t in range(2):
        x = w_ref[:, dq + t * LANES:dq + (t + 1) * LANES]
        swapped = pltpu.roll(x, SWA_HEAD_DIM, axis=1)
        for g, dup in enumerate((jnp.where(low, x, swapped), jnp.where(low, swapped, x))):
            c = dq + (SWA_KV_HEADS * t + g) * LANES
            wb_ref[:, c:c + LANES] = dup.astype(BF16)


def _sb_kernel(q_ref, k_ref, v_ref, o_ref, kx_ref, vx_ref, run_ref, acc_ref):
    n = pl.program_id(2)
    nkb = k_ref.shape[0] // BLOCK
    sub = SB_TILE // BLOCK
    low = lax.broadcasted_iota(jnp.int32, (nkb, BLOCK, LANES), 2) < SB_HEAD_DIM

    @pl.when(n == 0)
    def _():
        for src, dst in ((k_ref, kx_ref), (v_ref, vx_ref)):
            for pp in range(SB_PAIRS):
                x = src[:, pp * LANES:(pp + 1) * LANES].reshape(nkb, BLOCK, LANES)
                zero = jnp.zeros_like(x)
                dst[pp, :, :BLOCK, :] = jnp.where(low, x, zero)
                dst[pp, :, BLOCK:, :] = jnp.where(low, zero, x)

    q = q_ref[...] * jnp.asarray(SB_HEAD_DIM ** -0.5, BF16)
    tj = lax.broadcasted_iota(jnp.int32, (2 * BLOCK, 2 * BLOCK), 0) % BLOCK
    ts = lax.broadcasted_iota(jnp.int32, (2 * BLOCK, 2 * BLOCK), 1)
    tri = jnp.where((tj > ts) | (ts >= BLOCK), 1.0, 0.0).astype(BF16)
    ti = lax.broadcasted_iota(jnp.int32, (BLOCK, 2 * BLOCK), 0)
    si = lax.broadcasted_iota(jnp.int32, (BLOCK, 2 * BLOCK), 1) % BLOCK
    causal = si < ti
    run_ref[...] = jnp.zeros_like(run_ref)
    acc_ref[...] = jnp.zeros_like(acc_ref)

    def mask_first_block(x):
        head = jnp.where(causal, x[:BLOCK], 0.0)
        return head if x.shape[0] == BLOCK else jnp.concatenate([head, x[BLOCK:]], axis=0)

    def slabs(specs):
        chains = [(pp, kb, r0) for kb, r0, _ in specs for pp in range(SB_PAIRS)]
        masked = [diag for _, _, diag in specs for pp in range(SB_PAIRS)]
        cols = lambda pp: slice(pp * LANES, (pp + 1) * LANES)
        state = [None] * len(chains)

        def scores(i):
            pp, kb, r0 = chains[i]
            state[i] = lax.dot_general(q[r0:, cols(pp)], kx_ref[pp, kb], (((1,), (1,)), ((), ())),
                                       preferred_element_type=F32)

        def prefix(i):
            z = state[i]
            t = jnp.maximum(z, 0.0) + jnp.log(1.0 + jnp.exp2(jnp.abs(z) * -LOG2E))
            log_beta = z - t
            if masked[i]:
                t = mask_first_block(t)
            hi = t.astype(BF16)
            lo = (t - hi.astype(F32)).astype(BF16)
            c = [jnp.dot(jnp.concatenate([hi[:, s:s + BLOCK], lo[:, s:s + BLOCK]], axis=1), tri,
                         preferred_element_type=F32) for s in (0, BLOCK)]
            state[i] = (log_beta, c)

        def values(i):
            pp, kb, r0 = chains[i]
            log_beta, c = state[i]
            cum = jnp.concatenate([c[0][:, :BLOCK], c[1][:, :BLOCK]], axis=1)
            tot = jnp.concatenate([c[0][:, BLOCK:], c[1][:, BLOCK:]], axis=1)
            run = run_ref[pp, r0:, :]
            a = jnp.exp(log_beta - cum - run)
            if masked[i]:
                a = mask_first_block(a)
            acc_ref[r0:, cols(pp)] += jnp.dot(a.astype(BF16), vx_ref[pp, kb], preferred_element_type=F32)
            run_ref[pp, r0:, :] = run + tot
            state[i] = None

        _staggered((scores, prefix, values), len(chains))

    slabs([(n * sub + j, j * BLOCK, True) for j in reversed(range(sub))])

    def tile(carry):
        i, _ = carry
        kb0 = (n - 1 - i) * sub
        slabs([(kb0 + j, 0, False) for j in reversed(range(sub))])
        return i + 1, jnp.min(run_ref[...])

    lax.while_loop(lambda c: (c[0] < n) & (c[1] < SB_EXP_ZERO), tile, (jnp.int32(0), jnp.float32(0.0)))
    o_ref[...] = acc_ref[...].astype(o_ref.dtype)


def sb_attention(qkv, *, batch, seq):
    nt = seq // SB_TILE
    nkb = seq // BLOCK
    width = SB_PAIRS * LANES
    groups = SB_HEADS * SB_HEAD_DIM // width
    return pl.pallas_call(
        _sb_kernel,
        out_shape=jax.ShapeDtypeStruct((batch * seq, SB_HEADS * SB_HEAD_DIM), BF16),
        grid=(batch, groups, nt),
        in_specs=[
            pl.BlockSpec((SB_TILE, width), lambda b, p, n: (b * nt + n, p)),
            pl.BlockSpec((seq, width), lambda b, p, n: (b, groups + p)),
            pl.BlockSpec((seq, width), lambda b, p, n: (b, 2 * groups + p)),
        ],
        out_specs=pl.BlockSpec((SB_TILE, width), lambda b, p, n: (b * nt + n, p)),
        scratch_shapes=[
            pltpu.VMEM((SB_PAIRS, nkb, 2 * BLOCK, LANES), BF16),
            pltpu.VMEM((SB_PAIRS, nkb, 2 * BLOCK, LANES), BF16),
            pltpu.VMEM((SB_PAIRS, SB_TILE, 2 * LANES), F32),
            pltpu.VMEM((SB_TILE, width), F32),
        ],
        compiler_params=_params("parallel", "parallel", "arbitrary"),
        name="sb_attention",
    )(qkv, qkv, qkv)


def _ret_kernel(cd_ref, qkv_ref, g_ref, dm_ref, qd_ref, kd_ref, y_ref, state_ref):
    hh, dk, dv = RET_HEADS, RET_QK_DIM, RET_V_DIM

    @pl.when(pl.program_id(1) == 0)
    def _():
        state_ref[...] = jnp.zeros_like(state_ref)

    chains = [(slice(c * BLOCK, (c + 1) * BLOCK), h) for c in range(RET_STEP_CHUNKS) for h in range(hh)]
    carry = [None] * len(chains)
    value = lambda rows, h: qkv_ref[rows, 2 * hh * dk + h * dv:2 * hh * dk + (h + 1) * dv]

    def scores(i):
        rows, h = chains[i]
        q = qkv_ref[rows, h * dk:(h + 1) * dk]
        kf = qkv_ref[rows, (hh + h) * dk:(hh + h + 1) * dk].astype(F32) * (dk ** -0.5)
        s = lax.dot_general(q, kf.astype(BF16), (((1,), (1,)), ((), ())), preferred_element_type=F32) * dm_ref[h]
        qd = (q.astype(F32) * qd_ref[h]).astype(BF16)
        kd = (kf * kd_ref[h]).astype(BF16)
        carry[i] = (s.astype(BF16), qd, kd)

    def outputs(i):
        rows, h = chains[i]
        s, qd, kd = carry[i]
        v = value(rows, h)
        state = state_ref[h]
        carry[i] = jnp.dot(s, v, preferred_element_type=F32) + jnp.dot(qd, state.astype(BF16),
                                                                      preferred_element_type=F32)
        state_ref[h] = state * cd_ref[h] + lax.dot_general(kd, v, (((0,), (0,)), ((), ())),
                                                           preferred_element_type=F32)

    def gated(i):
        rows, h = chains[i]
        o = carry[i]
        o = o * lax.rsqrt(jnp.mean(o * o, axis=-1, keepdims=True) + RMS_EPS)
        gate = g_ref[rows, h * dv:(h + 1) * dv]
        y_ref[rows, h * dv:(h + 1) * dv] = (gate * (1.0 / (1.0 + jnp.exp(-gate))) * o).astype(y_ref.dtype)
        carry[i] = None

    _staggered((scores, outputs, gated), len(chains))


def _ret_decay_tables():
    c, h = BLOCK, RET_HEADS
    log_gamma = jnp.log1p(-jnp.exp2(-5.0 - jnp.arange(h, dtype=F32)))
    idx = jnp.arange(c)
    diff = idx[:, None] - idx[None, :]
    decay_mat = jnp.where(diff >= 0, jnp.exp(log_gamma[:, None, None] * jnp.maximum(diff, 0).astype(F32)), 0.0)
    q_decay = jnp.exp(log_gamma[:, None] * (idx + 1).astype(F32))[:, :, None]
    k_decay = jnp.exp(log_gamma[:, None] * (c - 1 - idx).astype(F32))[:, :, None]
    chunk_decay = jnp.exp(log_gamma * c)
    return decay_mat, q_decay, k_decay, chunk_decay


def retention(qkv, gate, *, batch, seq):
    rows = RET_STEP_CHUNKS * BLOCK
    nc = seq // rows
    hh, dk, dv = RET_HEADS, RET_QK_DIM, RET_V_DIM
    dm, qd, kd, cd = _ret_decay_tables()
    return pl.pallas_call(
        _ret_kernel,
        out_shape=jax.ShapeDtypeStruct((batch * seq, hh * dv), BF16),
        grid=(batch, nc),
        in_specs=[
            pl.BlockSpec(memory_space=pltpu.SMEM),
            pl.BlockSpec((rows, qkv.shape[1]), lambda b, n: (b * nc + n, 0)),
            pl.BlockSpec((rows, hh * dv), lambda b, n: (b * nc + n, 0)),
            _resident(dm.shape),
            _resident(qd.shape),
            _resident(kd.shape),
        ],
        out_specs=pl.BlockSpec((rows, hh * dv), lambda b, n: (b * nc + n, 0)),
        scratch_shapes=[pltpu.VMEM((hh, dk, dv), F32)],
        compiler_params=_params("parallel", "arbitrary"),
        name="retention",
    )(cd, qkv, gate, dm, qd, kd)


def kernel(x, attn_norm, mlp_norm, final_norm, swa_w_qkv, swa_sinks, swa_w_o, sb_w_qkv, sb_w_o, ret_w_in, ret_w_o,
           mlp_w_up, mlp_w_down):
    batch, seq, d = x.shape
    h = x.reshape(batch * seq, d)
    ret_qkv_dim = 2 * RET_HEADS * RET_QK_DIM + RET_HEADS * RET_V_DIM
    ret_w_in_b, mlp_w_up_b, mlp_w_down_b = ret_w_in.astype(BF16), mlp_w_up.astype(BF16), mlp_w_down.astype(BF16)
    w_o_b = [w.astype(BF16) for w in (swa_w_o, sb_w_o, ret_w_o)]
    for i in range(DEPTH):
        kind, j = i % N_MIXERS, i // N_MIXERS
        if kind == 0:
            qkv, = norm_proj(h, attn_norm[i], swa_w_qkv, j, [(0, SWA_QKV_COLS, BF16)],
                             prepare=_swa_prepare, prepared_cols=SWA_QKV_COLS)
            mix = swa_attention(qkv, swa_sinks[j], batch=batch, seq=seq)
        elif kind == 1:
            sb_cols = sb_w_qkv.shape[-1]
            qkv, = norm_proj(h, attn_norm[i], sb_w_qkv, j, [(0, sb_cols, BF16)],
                             prepare=_cast_weight, prepared_cols=sb_cols)
            mix = sb_attention(qkv, batch=batch, seq=seq)
        else:
            qkv, gate = norm_proj(h, attn_norm[i], ret_w_in_b, j,
                                  [(0, ret_qkv_dim, BF16), (ret_qkv_dim, ret_w_in.shape[-1], F32)])
            mix = retention(qkv, gate, batch=batch, seq=seq)
        h = layer_tail(mix, w_o_b[kind], j, h, mlp_norm[i], mlp_w_up_b, mlp_w_down_b, i,
                       final_gain=final_norm if i == DEPTH - 1 else None)
    return h.reshape(batch, seq, d)
```

```python
import functools

import jax
import jax.numpy as jnp
from jax import lax
from jax.experimental import pallas as pl
from jax.experimental.pallas import tpu as pltpu

D_MODEL = 1024
DEPTH = 4
N_MIXERS = 3
RMS_EPS = 1e-6
BLOCK = 128
NEG = -1e30

SWA_HEAD_DIM = 64
SWA_HEADS = 16
SWA_KV_HEADS = 2
SWA_GROUP = 8
SWA_STEP_BLOCKS = 8
SB_HEAD_DIM = 64
SB_HEADS = 16
SB_TILE = 256
SB_PAIRS = 4
SB_EXP_ZERO = 104.0
RET_QK_DIM = 256
RET_HEADS = 4
RET_V_DIM = 512
RET_STEP_CHUNKS = 4
D_FF = 4 * D_MODEL

LANES = 128
VMEM_LIMIT = 48 * 1024 * 1024
TOKEN_TILE = 512
COL_CHUNK = 1024
LOG2E = 1.4426950408889634

F32 = jnp.float32
BF16 = jnp.bfloat16


def _params(*sem):
    return pltpu.CompilerParams(dimension_semantics=sem, vmem_limit_bytes=VMEM_LIMIT)


def _rms(x, g):
    return x * lax.rsqrt(jnp.mean(x * x, axis=-1, keepdims=True) + RMS_EPS) * g


def _staggered(stages, n):
    for tick in range(n + len(stages) - 1):
        for depth, stage in enumerate(stages):
            if 0 <= tick - depth < n:
                stage(tick - depth)


def _resident(shape):
    return pl.BlockSpec(shape, lambda *_: (0,) * len(shape), pipeline_mode=pl.Buffered(1))


def _resident_layer(stack, layer):
    return pl.BlockSpec((None,) + stack.shape[1:], lambda *_: (layer, 0, 0), pipeline_mode=pl.Buffered(1))


def _norm_proj_kernel(outs, prepare, x_ref, g_ref, w_ref, *refs):
    o_refs = refs[:len(outs)]
    if prepare is not None:
        wb_ref = refs[len(outs)]

        @pl.when(pl.program_id(0) == 0)
        def _():
            prepare(w_ref, wb_ref)
    else:
        wb_ref = w_ref
    xn = _rms(x_ref[...], g_ref[...]).astype(BF16)
    for (c0, c1, _), o_ref in zip(outs, o_refs):
        for c in range(c0, c1, COL_CHUNK):
            ce = min(c + COL_CHUNK, c1)
            o_ref[:, c - c0:ce - c0] = jnp.dot(xn, wb_ref[:, c:ce], preferred_element_type=F32).astype(o_ref.dtype)


def norm_proj(x, g, w_stack, layer, outs, prepare=None, prepared_cols=None, tm=TOKEN_TILE):
    n, d = x.shape
    scratch = [] if prepare is None else [pltpu.VMEM((d, prepared_cols), BF16)]
    return pl.pallas_call(
        functools.partial(_norm_proj_kernel, tuple(outs), prepare),
        out_shape=[jax.ShapeDtypeStruct((n, c1 - c0), dt) for c0, c1, dt in outs],
        grid=(n // tm,),
        in_specs=[pl.BlockSpec((tm, d), lambda i: (i, 0)), _resident((1, d)), _resident_layer(w_stack, layer)],
        out_specs=[pl.BlockSpec((tm, c1 - c0), lambda i: (i, 0)) for c0, c1, _ in outs],
        scratch_shapes=scratch,
        compiler_params=_params("parallel" if prepare is None else "arbitrary"),
        name="norm_proj",
    )(x, g.reshape(1, d), w_stack)


def _cast_weight(w_ref, wb_ref):
    wb_ref[...] = w_ref[...].astype(BF16)


def _tail_kernel(final, mix_ref, wo_ref, h_ref, g_ref, wu_ref, wd_ref, *refs):
    if final:
        fg_ref, o_ref = refs
    else:
        o_ref, = refs
    h1 = h_ref[...] + jnp.dot(mix_ref[...], wo_ref[...], preferred_element_type=F32)
    xn = _rms(h1, g_ref[...]).astype(BF16)
    a = jnp.maximum(jnp.dot(xn, wu_ref[...], preferred_element_type=F32), 0.0)
    out = h1 + jnp.dot((a * a).astype(BF16), wd_ref[...], preferred_element_type=F32)
    o_ref[...] = _rms(out, fg_ref[...]) if final else out


def layer_tail(mix, w_o, o_layer, h, g, w_up, w_down, layer, final_gain=None):
    n, d = h.shape
    k = mix.shape[1]
    tm = TOKEN_TILE
    final = final_gain is not None
    in_specs = [
        pl.BlockSpec((tm, k), lambda i: (i, 0)),
        _resident_layer(w_o, o_layer),
        pl.BlockSpec((tm, d), lambda i: (i, 0)),
        _resident((1, d)),
        _resident_layer(w_up, layer),
        _resident_layer(w_down, layer),
    ]
    args = [mix, w_o, h, g.reshape(1, d), w_up, w_down]
    if final:
        in_specs.append(_resident((1, d)))
        args.append(final_gain.reshape(1, d))
    return pl.pallas_call(
        functools.partial(_tail_kernel, final),
        out_shape=jax.ShapeDtypeStruct((n, d), F32),
        grid=(n // tm,),
        in_specs=in_specs,
        out_specs=pl.BlockSpec((tm, d), lambda i: (i, 0)),
        compiler_params=_params("parallel"),
        name="layer_tail",
    )(*args)


def _swa_kernel(sink_ref, q_ref, kvp_ref, kvc_ref, o_ref):
    n = pl.program_id(1)
    q = q_ref[...] * jnp.asarray(SWA_HEAD_DIM ** -0.5, BF16)
    kv = jnp.concatenate([kvp_ref[...], kvc_ref[...]], axis=0)
    qi = lax.broadcasted_iota(jnp.int32, (BLOCK, BLOCK), 0)
    ci = lax.broadcasted_iota(jnp.int32, (BLOCK, BLOCK), 1)
    upper = ci > qi
    valid = jnp.logical_not(upper) | (n > 0)
    distf = jnp.where(upper, BLOCK + qi - ci, qi - ci).astype(F32)
    low = lax.broadcasted_iota(jnp.int32, (BLOCK, LANES), 1) < SWA_HEAD_DIM
    chains = SWA_STEP_BLOCKS * SWA_HEADS
    state = [None] * chains
    outs = [None] * chains

    def scores(i):
        c, h = divmod(i, SWA_HEADS)
        g, c0 = h // SWA_GROUP, (h - h % 2) * SWA_HEAD_DIM
        q2 = q[c * BLOCK:(c + 1) * BLOCK, c0:c0 + LANES]
        qm = jnp.where(low if h % 2 == 0 else jnp.logical_not(low), q2, jnp.zeros_like(q2))
        state[i] = lax.dot_general(qm, kv[c * BLOCK:(c + 2) * BLOCK, g * LANES:(g + 1) * LANES],
                                   (((1,), (1,)), ((), ())), preferred_element_type=F32)

    def softmax(i):
        c, h = divmod(i, SWA_HEADS)
        s = state[i]
        slope = 2.0 ** (-8.0 * (h + 1) / SWA_HEADS)
        logits = jnp.where(upper, s[:, :BLOCK], s[:, BLOCK:]) - slope * distf
        if c == 0:
            logits = jnp.where(valid, logits, NEG)
        sink = sink_ref[h]
        m = jnp.maximum(jnp.max(logits, axis=-1, keepdims=True), sink)
        pexp = jnp.exp(logits - m)
        inv = 1.0 / (jnp.sum(pexp, axis=-1, keepdims=True) + jnp.exp(sink - m))
        pb = pexp.astype(BF16)
        zero = jnp.zeros_like(pb)
        state[i] = (jnp.concatenate([jnp.where(upper, pb, zero), jnp.where(upper, zero, pb)], axis=1), inv)

    def values(i):
        c, h = divmod(i, SWA_HEADS)
        g, c0 = h // SWA_GROUP, (h - h % 2) * SWA_HEAD_DIM
        p2, inv = state[i]
        vv = kv[c * BLOCK:(c + 2) * BLOCK, (SWA_KV_HEADS + g) * LANES:(SWA_KV_HEADS + g + 1) * LANES]
        outs[i] = jnp.dot(p2, vv, preferred_element_type=F32) * inv
        state[i] = None
        if h % 2 == 1:
            o_ref[c * BLOCK:(c + 1) * BLOCK, c0:c0 + LANES] = jnp.where(low, outs[i - 1], outs[i]).astype(o_ref.dtype)
            outs[i - 1] = outs[i] = None

    _staggered((scores, softmax, values), chains)


def swa_attention(qkv, sinks, *, batch, seq):
    rows = SWA_STEP_BLOCKS * BLOCK
    nb = seq // rows
    dq = SWA_HEADS * SWA_HEAD_DIM
    dkv = 2 * SWA_KV_HEADS * LANES
    kv_col = dq // dkv
    return pl.pallas_call(
        _swa_kernel,
        out_shape=jax.ShapeDtypeStruct((batch * seq, dq), BF16),
        grid=(batch, nb),
        in_specs=[
            pl.BlockSpec(memory_space=pltpu.SMEM),
            pl.BlockSpec((rows, dq), lambda b, n: (b * nb + n, 0)),
            pl.BlockSpec((BLOCK, dkv), lambda b, n: ((b * nb + n) * SWA_STEP_BLOCKS - jnp.minimum(n, 1), kv_col)),
            pl.BlockSpec((rows, dkv), lambda b, n: (b * nb + n, kv_col)),
        ],
        out_specs=pl.BlockSpec((rows, dq), lambda b, n: (b * nb + n, 0)),
        compiler_params=_params("parallel", "arbitrary"),
        name="swa_attention",
    )(sinks, qkv, qkv, qkv)


SWA_QKV_COLS = SWA_HEADS * SWA_HEAD_DIM + 2 * SWA_KV_HEADS * LANES


def _swa_prepare(w_ref, wb_ref):
    dq = SWA_HEADS * SWA_HEAD_DIM
    wb_ref[:, :dq] = w_ref[:, :dq].astype(BF16)
    low = lax.broadcasted_iota(jnp.int32, (w_ref.shape[0], LANES), 1) < SWA_HEAD_DIM
    for t in range(2):
        x = w_ref[:, dq + t * LANES:dq + (t + 1) * LANES]
        swapped = pltpu.roll(x, SWA_HEAD_DIM, axis=1)
        for g, dup in enumerate((jnp.where(low, x, swapped), jnp.where(low, swapped, x))):
            c = dq + (SWA_KV_HEADS * t + g) * LANES
            wb_ref[:, c:c + LANES] = dup.astype(BF16)


def _sb_kernel(q_ref, k_ref, v_ref, o_ref, kx_ref, vx_ref, run_ref, acc_ref):
    n = pl.program_id(2)
    nkb = k_ref.shape[0] // BLOCK
    sub = SB_TILE // BLOCK
    low = lax.broadcasted_iota(jnp.int32, (nkb, BLOCK, LANES), 2) < SB_HEAD_DIM

    @pl.when(n == 0)
    def _():
        for src, dst in ((k_ref, kx_ref), (v_ref, vx_ref)):
            for pp in range(SB_PAIRS):
                x = src[:, pp * LANES:(pp + 1) * LANES].reshape(nkb, BLOCK, LANES)
                zero = jnp.zeros_like(x)
                dst[pp, :, :BLOCK, :] = jnp.where(low, x, zero)
                dst[pp, :, BLOCK:, :] = jnp.where(low, zero, x)

    q = q_ref[...] * jnp.asarray(SB_HEAD_DIM ** -0.5, BF16)
    tj = lax.broadcasted_iota(jnp.int32, (2 * BLOCK, 2 * BLOCK), 0) % BLOCK
    ts = lax.broadcasted_iota(jnp.int32, (2 * BLOCK, 2 * BLOCK), 1)
    tri = jnp.where((tj > ts) | (ts >= BLOCK), 1.0, 0.0).astype(BF16)
    ti = lax.broadcasted_iota(jnp.int32, (BLOCK, 2 * BLOCK), 0)
    si = lax.broadcasted_iota(jnp.int32, (BLOCK, 2 * BLOCK), 1) % BLOCK
    causal = si < ti
    run_ref[...] = jnp.zeros_like(run_ref)
    acc_ref[...] = jnp.zeros_like(acc_ref)

    def mask_first_block(x):
        head = jnp.where(causal, x[:BLOCK], 0.0)
        return head if x.shape[0] == BLOCK else jnp.concatenate([head, x[BLOCK:]], axis=0)

    def slabs(specs):
        chains = [(pp, kb, r0) for kb, r0, _ in specs for pp in range(SB_PAIRS)]
        masked = [diag for _, _, diag in specs for pp in range(SB_PAIRS)]
        cols = lambda pp: slice(pp * LANES, (pp + 1) * LANES)
        state = [None] * len(chains)

        def scores(i):
            pp, kb, r0 = chains[i]
            state[i] = lax.dot_general(q[r0:, cols(pp)], kx_ref[pp, kb], (((1,), (1,)), ((), ())),
                                       preferred_element_type=F32)

        def prefix(i):
            z = state[i]
            t = jnp.maximum(z, 0.0) + jnp.log(1.0 + jnp.exp2(jnp.abs(z) * -LOG2E))
            log_beta = z - t
            if masked[i]:
                t = mask_first_block(t)
            hi = t.astype(BF16)
            lo = (t - hi.astype(F32)).astype(BF16)
            c = [jnp.dot(jnp.concatenate([hi[:, s:s + BLOCK], lo[:, s:s + BLOCK]], axis=1), tri,
                         preferred_element_type=F32) for s in (0, BLOCK)]
            state[i] = (log_beta, c)

        def values(i):
            pp, kb, r0 = chains[i]
            log_beta, c = state[i]
            cum = jnp.concatenate([c[0][:, :BLOCK], c[1][:, :BLOCK]], axis=1)
            tot = jnp.concatenate([c[0][:, BLOCK:], c[1][:, BLOCK:]], axis=1)
            run = run_ref[pp, r0:, :]
            a = jnp.exp(log_beta - cum - run)
            if masked[i]:
                a = mask_first_block(a)
            acc_ref[r0:, cols(pp)] += jnp.dot(a.astype(BF16), vx_ref[pp, kb], preferred_element_type=F32)
            run_ref[pp, r0:, :] = run + tot
            state[i] = None

        _staggered((scores, prefix, values), len(chains))

    slabs([(n * sub + j, j * BLOCK, True) for j in reversed(range(sub))])

    def tile(carry):
        i, _ = carry
        kb0 = (n - 1 - i) * sub
        slabs([(kb0 + j, 0, False) for j in reversed(range(sub))])
        return i + 1, jnp.min(run_ref[...])

    lax.while_loop(lambda c: (c[0] < n) & (c[1] < SB_EXP_ZERO), tile, (jnp.int32(0), jnp.float32(0.0)))
    o_ref[...] = acc_ref[...].astype(o_ref.dtype)


def sb_attention(qkv, *, batch, seq):
    nt = seq // SB_TILE
    nkb = seq // BLOCK
    width = SB_PAIRS * LANES
    groups = SB_HEADS * SB_HEAD_DIM // width
    return pl.pallas_call(
        _sb_kernel,
        out_shape=jax.ShapeDtypeStruct((batch * seq, SB_HEADS * SB_HEAD_DIM), BF16),
        grid=(batch, groups, nt),
        in_specs=[
            pl.BlockSpec((SB_TILE, width), lambda b, p, n: (b * nt + n, p)),
            pl.BlockSpec((seq, width), lambda b, p, n: (b, groups + p)),
            pl.BlockSpec((seq, width), lambda b, p, n: (b, 2 * groups + p)),
        ],
        out_specs=pl.BlockSpec((SB_TILE, width), lambda b, p, n: (b * nt + n, p)),
        scratch_shapes=[
            pltpu.VMEM((SB_PAIRS, nkb, 2 * BLOCK, LANES), BF16),
            pltpu.VMEM((SB_PAIRS, nkb, 2 * BLOCK, LANES), BF16),
            pltpu.VMEM((SB_PAIRS, SB_TILE, 2 * LANES), F32),
            pltpu.VMEM((SB_TILE, width), F32),
        ],
        compiler_params=_params("parallel", "parallel", "arbitrary"),
        name="sb_attention",
    )(qkv, qkv, qkv)


def _ret_kernel(cd_ref, qkv_ref, g_ref, dm_ref, qd_ref, kd_ref, y_ref, state_ref):
    hh, dk, dv = RET_HEADS, RET_QK_DIM, RET_V_DIM

    @pl.when(pl.program_id(1) == 0)
    def _():
        state_ref[...] = jnp.zeros_like(state_ref)

    chains = [(slice(c * BLOCK, (c + 1) * BLOCK), h) for c in range(RET_STEP_CHUNKS) for h in range(hh)]
    carry = [None] * len(chains)
    value = lambda rows, h: qkv_ref[rows, 2 * hh * dk + h * dv:2 * hh * dk + (h + 1) * dv]

    def scores(i):
        rows, h = chains[i]
        q = qkv_ref[rows, h * dk:(h + 1) * dk]
        kf = qkv_ref[rows, (hh + h) * dk:(hh + h + 1) * dk].astype(F32) * (dk ** -0.5)
        s = lax.dot_general(q, kf.astype(BF16), (((1,), (1,)), ((), ())), preferred_element_type=F32) * dm_ref[h]
        qd = (q.astype(F32) * qd_ref[h]).astype(BF16)
        kd = (kf * kd_ref[h]).astype(BF16)
        carry[i] = (s.astype(BF16), qd, kd)

    def outputs(i):
        rows, h = chains[i]
        s, qd, kd = carry[i]
        v = value(rows, h)
        state = state_ref[h]
        carry[i] = jnp.dot(s, v, preferred_element_type=F32) + jnp.dot(qd, state.astype(BF16),
                                                                      preferred_element_type=F32)
        state_ref[h] = state * cd_ref[h] + lax.dot_general(kd, v, (((0,), (0,)), ((), ())),
                                                           preferred_element_type=F32)

    def gated(i):
        rows, h = chains[i]
        o = carry[i]
        o = o * lax.rsqrt(jnp.mean(o * o, axis=-1, keepdims=True) + RMS_EPS)
        gate = g_ref[rows, h * dv:(h + 1) * dv]
        y_ref[rows, h * dv:(h + 1) * dv] = (gate * (1.0 / (1.0 + jnp.exp(-gate))) * o).astype(y_ref.dtype)
        carry[i] = None

    _staggered((scores, outputs, gated), len(chains))


def _ret_decay_tables():
    c, h = BLOCK, RET_HEADS
    log_gamma = jnp.log1p(-jnp.exp2(-5.0 - jnp.arange(h, dtype=F32)))
    idx = jnp.arange(c)
    diff = idx[:, None] - idx[None, :]
    decay_mat = jnp.where(diff >= 0, jnp.exp(log_gamma[:, None, None] * jnp.maximum(diff, 0).astype(F32)), 0.0)
    q_decay = jnp.exp(log_gamma[:, None] * (idx + 1).astype(F32))[:, :, None]
    k_decay = jnp.exp(log_gamma[:, None] * (c - 1 - idx).astype(F32))[:, :, None]
    chunk_decay = jnp.exp(log_gamma * c)
    return decay_mat, q_decay, k_decay, chunk_decay


def retention(qkv, gate, *, batch, seq):
    rows = RET_STEP_CHUNKS * BLOCK
    nc = seq // rows
    hh, dk, dv = RET_HEADS, RET_QK_DIM, RET_V_DIM
    dm, qd, kd, cd = _ret_decay_tables()
    return pl.pallas_call(
        _ret_kernel,
        out_shape=jax.ShapeDtypeStruct((batch * seq, hh * dv), BF16),
        grid=(batch, nc),
        in_specs=[
            pl.BlockSpec(memory_space=pltpu.SMEM),
            pl.BlockSpec((rows, qkv.shape[1]), lambda b, n: (b * nc + n, 0)),
            pl.BlockSpec((rows, hh * dv), lambda b, n: (b * nc + n, 0)),
            _resident(dm.shape),
            _resident(qd.shape),
            _resident(kd.shape),
        ],
        out_specs=pl.BlockSpec((rows, hh * dv), lambda b, n: (b * nc + n, 0)),
        scratch_shapes=[pltpu.VMEM((hh, dk, dv), F32)],
        compiler_params=_params("parallel", "arbitrary"),
        name="retention",
    )(cd, qkv, gate, dm, qd, kd)


def kernel(x, attn_norm, mlp_norm, final_norm, swa_w_qkv, swa_sinks, swa_w_o, sb_w_qkv, sb_w_o, ret_w_in, ret_w_o,
           mlp_w_up, mlp_w_down):
    batch, seq, d = x.shape
    h = x.reshape(batch * seq, d)
    ret_qkv_dim = 2 * RET_HEADS * RET_QK_DIM + RET_HEADS * RET_V_DIM
    ret_w_in_b, mlp_w_up_b, mlp_w_down_b = ret_w_in.astype(BF16), mlp_w_up.astype(BF16), mlp_w_down.astype(BF16)
    w_o_b = [w.astype(BF16) for w in (swa_w_o, sb_w_o, ret_w_o)]
    for i in range(DEPTH):
        kind, j = i % N_MIXERS, i // N_MIXERS
        if kind == 0:
            qkv, = norm_proj(h, attn_norm[i], swa_w_qkv, j, [(0, SWA_QKV_COLS, BF16)],
                             prepare=_swa_prepare, prepared_cols=SWA_QKV_COLS, tm=2 * TOKEN_TILE)
            mix = swa_attention(qkv, swa_sinks[j], batch=batch, seq=seq)
        elif kind == 1:
            sb_cols = sb_w_qkv.shape[-1]
            qkv, = norm_proj(h, attn_norm[i], sb_w_qkv, j, [(0, sb_cols, BF16)],
                             prepare=_cast_weight, prepared_cols=sb_cols, tm=2 * TOKEN_TILE)
            mix = sb_attention(qkv, batch=batch, seq=seq)
        else:
            qkv, gate = norm_proj(h, attn_norm[i], ret_w_in_b, j,
                                  [(0, ret_qkv_dim, BF16), (ret_qkv_dim, ret_w_in.shape[-1], F32)])
            mix = retention(qkv, gate, batch=batch, seq=seq)
        h = layer_tail(mix, w_o_b[kind], j, h, mlp_norm[i], mlp_w_up_b, mlp_w_down_b, i,
                       final_gain=final_norm if i == DEPTH - 1 else None)
    return h.reshape(batch, seq, d)
```
